```python
import math
import jax, jax.numpy as jnp
from jax import lax
import numpy as np

D_MODEL = 1024
BATCH = 8
SEQ = 2048
DEPTH = 2
DEC_BATCH = 32
DEC_SEQ = 1
PAST_LEN = 16384
PAGE_SIZE = 128

F32 = jnp.float32
EPS = 1e-6
N_EVEN = (DEPTH + 1) // 2
N_ODD = DEPTH // 2
H_RET = 4
DK_RET = D_MODEL // 8
DV_RET = 2 * DK_RET
RET_CHUNK = 128
H_DIFF = 4
DH_DIFF = D_MODEL // 16
DV_DIFF = 2 * DH_DIFF
Q_BLOCK = 128
ROPE_THETA = 10000.0
RET_QK = H_RET * DK_RET
RET_V = H_RET * DV_RET
DIFF_QK = H_DIFF * 2 * DH_DIFF
DIFF_V = H_DIFF * DV_DIFF
EVEN_IN = 2 * RET_QK + 2 * RET_V + 2 * DIFF_QK + DIFF_V
EVEN_MIX = RET_V + DIFF_V
EVEN_SPLITS = (RET_QK, 2 * RET_QK, 2 * RET_QK + RET_V, 2 * RET_QK + 2 * RET_V,
               2 * RET_QK + 2 * RET_V + DIFF_QK, 2 * RET_QK + 2 * RET_V + 2 * DIFF_QK)
GMLP_E = 3 * D_MODEL
GMLP_GROUPS = 8
GMLP_CHUNK = 128
D_FF = 4 * D_MODEL

kernel_name = "hybrid_retention_diffattn_gmlp_decode_step"


def rms_norm(x, g):
    xf = x.astype(F32)
    y = xf * lax.rsqrt(jnp.mean(xf * xf, axis=-1, keepdims=True) + EPS)
    return (y * g.astype(F32)).astype(x.dtype)


def layer_norm(x, g, b):
    xf = x.astype(F32)
    xc = xf - jnp.mean(xf, axis=-1, keepdims=True)
    var = jnp.mean(xc * xc, axis=-1, keepdims=True)
    return (xc * lax.rsqrt(var + EPS) * g.astype(F32) + b.astype(F32)).astype(x.dtype)


def head_group_norm(o):
    oc = o - jnp.mean(o, axis=-1, keepdims=True)
    return oc * lax.rsqrt(jnp.mean(oc * oc, axis=-1, keepdims=True) + EPS)


def rope(x, pos):
    d = x.shape[-1]
    inv = 1.0 / (ROPE_THETA ** (jnp.arange(0, d, 2, dtype=F32) / d))
    ang = pos.astype(F32)[:, None] * inv[None, :]
    cos = jnp.cos(ang)[:, None, :]
    sin = jnp.sin(ang)[:, None, :]
    xf = x.astype(F32)
    x1, x2 = xf[..., : d // 2], xf[..., d // 2:]
    return jnp.concatenate([x1 * cos - x2 * sin, x2 * cos + x1 * sin], axis=-1).astype(x.dtype)


def retention(q, k, v, s0):
    B, L, H, dk = q.shape
    dv = v.shape[-1]
    C = RET_CHUNK if L % RET_CHUNK == 0 else L
    nC = L // C
    log_g = jnp.log1p(-jnp.exp2(-5.0 - jnp.arange(H, dtype=F32)))
    qf = q.astype(F32).reshape(B, nC, C, H, dk)
    kf = k.astype(F32).reshape(B, nC, C, H, dk) * (dk ** -0.5)
    vf = v.astype(F32).reshape(B, nC, C, H, dv)
    idx = jnp.arange(C, dtype=F32)
    diff = idx[:, None] - idx[None, :]
    causal = diff >= 0
    dmask = jnp.where(causal[None], jnp.exp(jnp.where(causal, diff, 0.0)[None] * log_g[:, None, None]), 0.0)
    scores = jnp.einsum('bnihd,bnjhd->bnhij', qf, kf) * dmask
    inner = jnp.einsum('bnhij,bnjhe->bnihe', scores, vf)
    zeta = jnp.exp((C - 1 - idx)[None, :] * log_g[:, None])
    kv = jnp.einsum('bnjhd,hj,bnjhe->bnhde', kf, zeta, vf)
    g_chunk = jnp.exp(C * log_g)[None, :, None, None]

    def step(s, kv_c):
        return g_chunk * s + kv_c, s

    s_fin, s_prev = lax.scan(step, s0.astype(F32), jnp.moveaxis(kv, 1, 0))
    s_prev = jnp.moveaxis(s_prev, 0, 1)
    xi = jnp.exp((idx + 1)[None, :] * log_g[:, None])
    cross = jnp.einsum('bnihd,bnhde,hi->bnihe', qf, s_prev, xi)
    return (inner + cross).reshape(B, L, H, dv), s_fin


def diff_attn(q, k, v, q_pos, k_pos, lam):
    B, Lq, H, _, dh = q.shape
    qb = Q_BLOCK if Lq % Q_BLOCK == 0 else Lq
    nb = Lq // qb
    kf = k.astype(F32)
    vf = v.astype(F32)
    scale = dh ** -0.5
    q_blocks = jnp.moveaxis(q.reshape(B, nb, qb, H, 2, dh), 1, 0)
    pos_blocks = q_pos.reshape(nb, qb)

    def block(args):
        qi, pi = args
        s = jnp.einsum('bqhmd,bkhmd->bhmqk', qi.astype(F32), kf) * scale
        s = jnp.where(k_pos[None, :] <= pi[:, None], s, -jnp.inf)
        p = jax.nn.softmax(s, axis=-1)
        a = p[:, :, 0] - lam * p[:, :, 1]
        return jnp.einsum('bhqk,bkhe->bqhe', a, vf)

    o = lax.map(block, (q_blocks, pos_blocks))
    return jnp.moveaxis(o, 0, 1).reshape(B, Lq, H, v.shape[-1])


def even_mixer(h, pos, s0, k_past, v_past, w_in, w_out, qg, kg, lq1, lk1, lq2, lk2, sub_g, lam_init):
    B, L, _ = h.shape
    proj = h @ w_in
    rq, rk, rv, rg, dq, dk, dv = jnp.split(proj, EVEN_SPLITS, axis=-1)
    rq = rope(rq.reshape(B, L, H_RET, DK_RET), pos)
    rk = rope(rk.reshape(B, L, H_RET, DK_RET), pos)
    ro, s_fin = retention(rq, rk, rv.reshape(B, L, H_RET, DV_RET), s0)
    ro = head_group_norm(ro).reshape(B, L, RET_V) * jax.nn.silu(rg.astype(F32))
    dq = rope(rms_norm(dq.reshape(B, L, 2 * H_DIFF, DH_DIFF), qg), pos).reshape(B, L, H_DIFF, 2, DH_DIFF)
    dk = rope(rms_norm(dk.reshape(B, L, 2 * H_DIFF, DH_DIFF), kg), pos).reshape(B, L, H_DIFF, 2, DH_DIFF)
    dv = dv.reshape(B, L, H_DIFF, DV_DIFF)
    if k_past is None:
        k_all, v_all, k_pos = dk, dv, pos
    else:
        k_all = jnp.concatenate([k_past.reshape(B, -1, H_DIFF, 2, DH_DIFF).astype(dk.dtype), dk], axis=1)
        v_all = jnp.concatenate([v_past.astype(dv.dtype), dv], axis=1)
        k_pos = jnp.arange(k_all.shape[1], dtype=jnp.int32)
    lam = (jnp.exp(jnp.sum(lq1.astype(F32) * lk1.astype(F32)))
           - jnp.exp(jnp.sum(lq2.astype(F32) * lk2.astype(F32))) + lam_init)
    do = diff_attn(dq, k_all, v_all, pos, k_pos, lam)
    do = rms_norm(do, sub_g) * (1.0 - lam_init)
    mix = jnp.concatenate([ro.astype(h.dtype), do.astype(h.dtype).reshape(B, L, DIFF_V)], axis=-1) @ w_out
    return mix, s_fin, dk.reshape(B, L, H_DIFF, 2 * DH_DIFF), dv


def chunk_gmlp(h, w_in, ln_g, ln_b, w_s, b_s, w_out):
    B, L, _ = h.shape
    z = jax.nn.gelu(h @ w_in, approximate=False)
    u, v = jnp.split(z, 2, axis=-1)
    v = layer_norm(v, ln_g, ln_b)
    n_chunks = -(-L // GMLP_CHUNK)
    pad = n_chunks * GMLP_CHUNK - L
    vp = jnp.pad(v, ((0, 0), (0, pad), (0, 0))).reshape(
        B, n_chunks, GMLP_CHUNK, GMLP_GROUPS, GMLP_E // GMLP_GROUPS)
    tri = jnp.tril(jnp.ones((GMLP_CHUNK, GMLP_CHUNK), dtype=bool))
    ws = jnp.where(tri[None], w_s, 0)
    s = jnp.einsum('gts,bnsgc->bntgc', ws, vp) + jnp.swapaxes(b_s, 0, 1)[:, :, None]
    s = s.reshape(B, n_chunks * GMLP_CHUNK, GMLP_E)[:, :L]
    return (u * s) @ w_out, v


def ffn(y, g, w_up, w_down):
    hid = rms_norm(y, g) @ w_up
    return y + jnp.square(jax.nn.relu(hid)) @ w_down


def setup_inputs(seed: int = 0) -> dict:
    key = jax.random.key(seed)
    ks = jax.random.split(key, 32)
    n_pages = PAST_LEN // PAGE_SIZE
    n_used = DEC_BATCH * n_pages
    n_pool = n_used + n_used // 4

    def nrm(k, shape, scale):
        return jax.random.normal(k, shape, F32) * scale

    perm = jax.random.permutation(ks[0], n_pool)
    page_table = perm[:n_used].reshape(DEC_BATCH, n_pages).astype(jnp.int32)
    return {
        "x_prompt": nrm(ks[1], (BATCH, SEQ, D_MODEL), 1.0),
        "x_sample": nrm(ks[2], (DEC_BATCH, DEC_SEQ, D_MODEL), 1.0),
        "state_ret": nrm(ks[3], (N_EVEN, DEC_BATCH, H_RET, DK_RET, DV_RET), 1.0),
        "cache_k": nrm(ks[4], (N_EVEN, n_pool, PAGE_SIZE, H_DIFF, 2 * DH_DIFF), 1.0),
        "cache_v": nrm(ks[5], (N_EVEN, n_pool, PAGE_SIZE, H_DIFF, DV_DIFF), 1.0),
        "page_table": page_table,
        "rms_mix_g": 1.0 + nrm(ks[6], (DEPTH, D_MODEL), 0.02),
        "rms_ffn_g": 1.0 + nrm(ks[7], (DEPTH, D_MODEL), 0.02),
        "w_ffn_up": nrm(ks[8], (DEPTH, D_MODEL, D_FF), D_MODEL ** -0.5),
        "w_ffn_down": nrm(ks[9], (DEPTH, D_FF, D_MODEL), 0.5 * D_FF ** -0.5),
        "w_in_even": nrm(ks[10], (N_EVEN, D_MODEL, EVEN_IN), D_MODEL ** -0.5),
        "w_out_even": nrm(ks[11], (N_EVEN, EVEN_MIX, D_MODEL), EVEN_MIX ** -0.5),
        "q_norm_g": 1.0 + nrm(ks[12], (N_EVEN, DH_DIFF), 0.02),
        "k_norm_g": 1.0 + nrm(ks[13], (N_EVEN, DH_DIFF), 0.02),
        "lam_q1": nrm(ks[14], (N_EVEN, DH_DIFF), 0.1),
        "lam_k1": nrm(ks[15], (N_EVEN, DH_DIFF), 0.1),
        "lam_q2": nrm(ks[16], (N_EVEN, DH_DIFF), 0.1),
        "lam_k2": nrm(ks[17], (N_EVEN, DH_DIFF), 0.1),
        "subln_g": 1.0 + nrm(ks[18], (N_EVEN, DV_DIFF), 0.02),
        "w_in_odd": nrm(ks[19], (N_ODD, D_MODEL, 2 * GMLP_E), D_MODEL ** -0.5),
        "ln_v_g": 1.0 + nrm(ks[20], (N_ODD, GMLP_E), 0.02),
        "ln_v_b": nrm(ks[21], (N_ODD, GMLP_E), 0.02),
        "w_s": nrm(ks[22], (N_ODD, GMLP_GROUPS, GMLP_CHUNK, GMLP_CHUNK), 0.5 * GMLP_CHUNK ** -0.5),
        "b_s": 1.0 + nrm(ks[23], (N_ODD, GMLP_GROUPS, GMLP_CHUNK), 0.1),
        "w_out_odd": nrm(ks[24], (N_ODD, GMLP_E, D_MODEL), GMLP_E ** -0.5),
    }


def reference(x_prompt, x_sample, state_ret, cache_k, cache_v, page_table,
              rms_mix_g, rms_ffn_g, w_ffn_up, w_ffn_down,
              w_in_even, w_out_even, q_norm_g, k_norm_g, lam_q1, lam_k1, lam_q2, lam_k2, subln_g,
              w_in_odd, ln_v_g, ln_v_b, w_s, b_s, w_out_odd):
    Bp, Sp, _ = x_prompt.shape
    Bs, Ls, _ = x_sample.shape
    past = page_table.shape[1] * cache_k.shape[2]
    pos_p = jnp.arange(Sp, dtype=jnp.int32)
    pos_s = past + jnp.arange(Ls, dtype=jnp.int32)
    yp, ys = x_prompt, x_sample
    ret_p, ret_s, k_p, v_p, k_s, v_s, gv_s = [], [], [], [], [], [], []
    for l in range(DEPTH):
        hp = rms_norm(yp, rms_mix_g[l])
        hs = rms_norm(ys, rms_mix_g[l])
        if l % 2 == 0:
            e = l // 2
            lam_init = 0.8 - 0.6 * math.exp(-0.3 * l)
            wts = (w_in_even[e], w_out_even[e], q_norm_g[e], k_norm_g[e],
                   lam_q1[e], lam_k1[e], lam_q2[e], lam_k2[e], subln_g[e], lam_init)
            s0 = jnp.zeros((Bp, H_RET, DK_RET, DV_RET), F32)
            mp, sp_fin, kp, vp = even_mixer(hp, pos_p, s0, None, None, *wts)
            k_past = cache_k[e][page_table].reshape(Bs, past, H_DIFF, 2 * DH_DIFF)
            v_past = cache_v[e][page_table].reshape(Bs, past, H_DIFF, DV_DIFF)
            ms, ss_fin, ks_new, vs_new = even_mixer(hs, pos_s, state_ret[e], k_past, v_past, *wts)
            ret_p.append(sp_fin.astype(state_ret.dtype))
            ret_s.append(ss_fin.astype(state_ret.dtype))
            k_p.append(kp)
            v_p.append(vp)
            k_s.append(ks_new)
            v_s.append(vs_new)
        else:
            o = l // 2
            mp, _ = chunk_gmlp(hp, w_in_odd[o], ln_v_g[o], ln_v_b[o], w_s[o], b_s[o], w_out_odd[o])
            ms, gs = chunk_gmlp(hs, w_in_odd[o], ln_v_g[o], ln_v_b[o], w_s[o], b_s[o], w_out_odd[o])
            gv_s.append(gs)
        yp = ffn(yp + mp, rms_ffn_g[l], w_ffn_up[l], w_ffn_down[l])
        ys = ffn(ys + ms, rms_ffn_g[l], w_ffn_up[l], w_ffn_down[l])
    return (yp, ys, jnp.stack(ret_p), jnp.stack(ret_s), jnp.stack(k_p), jnp.stack(v_p),
            jnp.stack(k_s), jnp.stack(v_s), jnp.stack(gv_s))
```

```python
import functools
import math

import jax
import jax.numpy as jnp
from jax import lax
from jax.experimental import pallas as pl
from jax.experimental.pallas import tpu as pltpu

F32 = jnp.float32
BF16 = jnp.bfloat16
EPS = 1e-6
ROPE_THETA = 10000.0
H_RET = 4
H_DIFF = 4
GMLP_GROUPS = 8
GMLP_CHUNK = 128
LANES = 128
VMEM_LIMIT_BYTES = 56 * 1024 * 1024
NEG_BIG = -1e30

ROW_TILE = 512
GMLP_TILE = 256
RET_CHUNK = 256
ATTN_TILE = 256
FF_CHUNK = 512
DEC_PAGES = 8


def _params(semantics):
    return pltpu.CompilerParams(dimension_semantics=semantics, vmem_limit_bytes=VMEM_LIMIT_BYTES)


def _resident(shape):
    return pl.BlockSpec(shape, lambda *_: (0,) * len(shape), pipeline_mode=pl.Buffered(1))


def _rms(x, g):
    return x * lax.rsqrt(jnp.mean(x * x, axis=-1, keepdims=True) + EPS) * g


def _log_gamma(h):
    return math.log1p(-(2.0 ** (-5.0 - h)))


def _in_proj_body(x_ref, g_ref, w_ref, cr_ref, sr_ref, cd_ref, sd_ref, qg_ref, kg_ref, grp_ref,
                  rq_ref, rk_ref, rv_ref, gate_ref, dq_ref, dk_ref, dv_ref, dkb_ref, dvb_ref,
                  *, ret_qk, ret_v, diff_qk, diff_v, dh):
    hb = _rms(x_ref[...], g_ref[...]).astype(BF16)
    rows = hb.shape[0]

    def proj(lo, width):
        return jnp.dot(hb, w_ref[:, lo:lo + width], preferred_element_type=F32)

    cr, sr = cr_ref[...], sr_ref[...]
    for col0, out_ref in ((0, rq_ref), (ret_qk, rk_ref)):
        p = proj(col0, ret_qk)
        for c in range(ret_qk // LANES):
            s = p[:, c * LANES:(c + 1) * LANES]
            out_ref[:, c * LANES:(c + 1) * LANES] = (s * cr + pltpu.roll(s, LANES // 2, 1) * sr).astype(BF16)
    rv_ref[...] = proj(2 * ret_qk, ret_v).astype(BF16)
    rg = proj(2 * ret_qk + ret_v, ret_v)
    gate_ref[...] = (rg * jax.nn.sigmoid(rg)).astype(BF16)

    cd, sd = cd_ref[...], sd_ref[...]
    lane = lax.broadcasted_iota(jnp.int32, (rows, LANES), 1)
    first_half = (lane & (dh // 2)) == 0
    grp = grp_ref[...]
    gw = grp.shape[0]
    base = 2 * ret_qk + 2 * ret_v

    def norm_rope(col0, gain):
        p = proj(col0, diff_qk)
        slabs = []
        for c in range(diff_qk // gw):
            s = p[:, c * gw:(c + 1) * gw]
            ssq = jnp.dot((s * s).astype(BF16), grp, preferred_element_type=F32)
            n = s * lax.rsqrt(ssq * (1.0 / dh) + EPS)
            for d in range(gw // LANES):
                t = n[:, d * LANES:(d + 1) * LANES] * gain
                partner = jnp.where(first_half, pltpu.roll(t, LANES - dh // 2, 1), pltpu.roll(t, dh // 2, 1))
                slabs.append(t * cd + partner * sd)
        return slabs

    q_scale = dh ** -0.5
    for c, s in enumerate(norm_rope(base, qg_ref[...])):
        dq_ref[:, c * LANES:(c + 1) * LANES] = (s * q_scale).astype(BF16)
    for c, s in enumerate(norm_rope(base + diff_qk, kg_ref[...])):
        dk_ref[:, c * LANES:(c + 1) * LANES] = s
        dkb_ref[:, c * LANES:(c + 1) * LANES] = s.astype(BF16)
    dv = proj(base + 2 * diff_qk, diff_v)
    dv_ref[...] = dv
    dvb_ref[...] = dv.astype(BF16)


def _in_proj(x, g, w, tabs, tab_blocks, qg, kg, grp, tm, dims):
    rows, d = x.shape
    ret_qk, ret_v, diff_qk, diff_v, dh = dims
    n = rows // tm
    row = lambda width: pl.BlockSpec((tm, width), lambda i: (i, 0))
    tab = pl.BlockSpec((tm, LANES), lambda i: (i % tab_blocks, 0))
    body = functools.partial(_in_proj_body, ret_qk=ret_qk, ret_v=ret_v, diff_qk=diff_qk, diff_v=diff_v, dh=dh)
    shapes = [((rows, ret_qk), BF16), ((rows, ret_qk), BF16), ((rows, ret_v), BF16), ((rows, ret_v), BF16),
              ((rows, diff_qk), BF16), ((rows, diff_qk), F32), ((rows, diff_v), F32),
              ((rows, diff_qk), BF16), ((rows, diff_v), BF16)]
    return pl.pallas_call(
        body,
        grid=(n,),
        in_specs=[row(d), _resident(g.shape), _resident(w.shape), tab, tab, tab, tab,
                  _resident(qg.shape), _resident(kg.shape), _resident(grp.shape)],
        out_specs=[row(s[1]) for s, _ in shapes],
        out_shape=[jax.ShapeDtypeStruct(s, t) for s, t in shapes],
        compiler_params=_params(("parallel",)),
        name="in_proj",
    )(x, g, w, *tabs, qg, kg, grp)


def _retention_body(q_ref, k_ref, v_ref, gate_ref, ro_ref, sfin_ref, state, dmask, *, dk, dv):
    b, c = pl.program_id(0), pl.program_id(1)
    chunk = q_ref.shape[0]
    scale = dk ** -0.5

    @pl.when((b == 0) & (c == 0))
    def _():
        i = lax.broadcasted_iota(jnp.int32, (chunk, chunk), 0)
        j = lax.broadcasted_iota(jnp.int32, (chunk, chunk), 1)
        causal = i >= j
        diff = jnp.where(causal, i - j, 0).astype(F32)
        for h in range(H_RET):
            dmask[h] = jnp.where(causal, jnp.exp(diff * _log_gamma(h)), 0.0) * scale

    @pl.when(c == 0)
    def _():
        state[...] = jnp.zeros_like(state)

    idx = lax.broadcasted_iota(jnp.int32, (chunk, 1), 0).astype(F32)
    for h in range(H_RET):
        lg = _log_gamma(h)
        q = q_ref[:, h * dk:(h + 1) * dk]
        k = k_ref[:, h * dk:(h + 1) * dk]
        v = v_ref[:, h * dv:(h + 1) * dv]
        s_prev = state[h]
        scores = lax.dot_general(q, k, (((1,), (1,)), ((), ())), preferred_element_type=F32) * dmask[h]
        inner = jnp.dot(scores.astype(BF16), v, preferred_element_type=F32)
        cross = jnp.dot(q, s_prev.astype(BF16), preferred_element_type=F32) * jnp.exp((idx + 1.0) * lg)
        o = inner + cross
        zeta = jnp.exp((chunk - 1.0 - idx) * lg) * scale
        kz_t = (k.astype(F32) * zeta).T.astype(BF16)
        state[h] = math.exp(chunk * lg) * s_prev + jnp.dot(kz_t, v, preferred_element_type=F32)
        oc = o - jnp.mean(o, axis=-1, keepdims=True)
        on = oc * lax.rsqrt(jnp.mean(oc * oc, axis=-1, keepdims=True) + EPS)
        ro_ref[:, h * dv:(h + 1) * dv] = (on * gate_ref[:, h * dv:(h + 1) * dv].astype(F32)).astype(BF16)

    @pl.when(c == pl.num_programs(1) - 1)
    def _():
        sfin_ref[0] = state[...]


def _retention_prompt(rq, rk, rv, gate, batch, seq, dk, dv):
    chunk = RET_CHUNK if seq % RET_CHUNK == 0 else seq
    nc = seq // chunk
    blk = lambda width: pl.BlockSpec((chunk, width), lambda b, c: (b * nc + c, 0))
    return pl.pallas_call(
        functools.partial(_retention_body, dk=dk, dv=dv),
        grid=(batch, nc),
        in_specs=[blk(H_RET * dk), blk(H_RET * dk), blk(H_RET * dv), blk(H_RET * dv)],
        out_specs=[blk(H_RET * dv), pl.BlockSpec((1, H_RET, dk, dv), lambda b, c: (b, 0, 0, 0))],
        out_shape=[jax.ShapeDtypeStruct((batch * seq, H_RET * dv), BF16),
                   jax.ShapeDtypeStruct((batch, H_RET, dk, dv), F32)],
        scratch_shapes=[pltpu.VMEM((H_RET, dk, dv), F32), pltpu.VMEM((H_RET, chunk, chunk), F32)],
        compiler_params=_params(("arbitrary", "arbitrary")),
        name="retention_prompt",
    )(rq, rk, rv, gate)


def _retention_step_body(s_ref, q_ref, k_ref, v_ref, gate_ref, snew_ref, ro_ref, *, dk):
    scale = dk ** -0.5
    for h in range(H_RET):
        s_new = math.exp(_log_gamma(h)) * s_ref[:, h] + (k_ref[:, h] * scale) * v_ref[:, h]
        snew_ref[:, h] = s_new
        o = jnp.sum(q_ref[:, h] * s_new, axis=1, keepdims=True)
        oc = o - jnp.mean(o, axis=-1, keepdims=True)
        on = oc * lax.rsqrt(jnp.mean(oc * oc, axis=-1, keepdims=True) + EPS)
        ro_ref[:, h] = (on * gate_ref[:, h].astype(F32)).astype(BF16)


def _retention_step(state, q_col, k_col, v_row, gate_row, bt):
    nb, _, dk, dv = state.shape
    col = pl.BlockSpec((bt, H_RET, dk, 1), lambda i: (i, 0, 0, 0))
    rowb = pl.BlockSpec((bt, H_RET, 1, dv), lambda i: (i, 0, 0, 0))
    st = pl.BlockSpec((bt, H_RET, dk, dv), lambda i: (i, 0, 0, 0))
    return pl.pallas_call(
        functools.partial(_retention_step_body, dk=dk),
        grid=(nb // bt,),
        in_specs=[st, col, col, rowb, rowb],
        out_specs=[st, rowb],
        out_shape=[jax.ShapeDtypeStruct(state.shape, F32), jax.ShapeDtypeStruct((nb, H_RET, 1, dv), BF16)],
        compiler_params=_params(("parallel",)),
        name="retention_step",
    )(state, q_col, k_col, v_row, gate_row)


def _lambda(lq1_ref, lk1_ref, lq2_ref, lk2_ref, lam_init):
    a = jnp.sum(lq1_ref[...] * lk1_ref[...], axis=-1, keepdims=True)
    b = jnp.sum(lq2_ref[...] * lk2_ref[...], axis=-1, keepdims=True)
    return jnp.exp(a) - jnp.exp(b) + lam_init


def _sub_norm(o, subg, lam_init):
    return o * lax.rsqrt(jnp.mean(o * o, axis=-1, keepdims=True) + EPS) * subg * (1.0 - lam_init)


def _attn_prompt_body(lq1_ref, lk1_ref, lq2_ref, lk2_ref, subg_ref, q_ref, k_ref, v_ref, o_ref, *, dh, lam_init):
    i = pl.program_id(2)
    t = q_ref.shape[0]
    q = q_ref[...]
    lane = lax.broadcasted_iota(jnp.int32, q.shape, 1)
    zero = jnp.zeros_like(q)
    qs = (jnp.where(lane < dh, q, zero), jnp.where(lane >= dh, q, zero))
    row = lax.broadcasted_iota(jnp.int32, (t, t), 0)
    col = lax.broadcasted_iota(jnp.int32, (t, t), 1)
    on_or_below = col <= row

    def block(j, carry, diagonal):
        kb = k_ref[pl.ds(pl.multiple_of(j * t, t), t), :]
        vb = v_ref[pl.ds(pl.multiple_of(j * t, t), t), :]
        out = []
        for m in range(2):
            mx, l, acc = carry[m]
            s = lax.dot_general(qs[m], kb, (((1,), (1,)), ((), ())), preferred_element_type=F32)
            if diagonal:
                s = jnp.where(on_or_below, s, NEG_BIG)
            mn = jnp.maximum(mx, jnp.max(s, axis=-1, keepdims=True))
            alpha = jnp.exp(mx - mn)
            p = jnp.exp(s - mn)
            l = alpha * l + jnp.sum(p, axis=-1, keepdims=True)
            acc = alpha * acc + jnp.dot(p.astype(BF16), vb, preferred_element_type=F32)
            out.append((mn, l, acc))
        return tuple(out)

    init = tuple((jnp.full((t, 1), NEG_BIG, F32), jnp.zeros((t, 1), F32), jnp.zeros((t, v_ref.shape[1]), F32))
                 for _ in range(2))
    carry = lax.fori_loop(0, i, lambda j, c: block(j, c, False), init)
    (_, l1, a1), (_, l2, a2) = block(i, carry, True)
    lam = _lambda(lq1_ref, lk1_ref, lq2_ref, lk2_ref, lam_init)
    o = a1 / l1 - lam * (a2 / l2)
    o_ref[...] = _sub_norm(o, subg_ref[...], lam_init).astype(BF16)


def _attn_prompt(dq, dkb, dvb, lam_vecs, subg, batch, seq, dh, lam_init):
    t = ATTN_TILE if seq % ATTN_TILE == 0 else seq
    nq = seq // t
    width = 2 * dh
    vec = [_resident(v.shape) for v in lam_vecs]
    return pl.pallas_call(
        functools.partial(_attn_prompt_body, dh=dh, lam_init=lam_init),
        grid=(batch, H_DIFF, nq),
        in_specs=vec + [_resident(subg.shape),
                        pl.BlockSpec((t, width), lambda b, h, i: (b * nq + i, h)),
                        pl.BlockSpec((seq, width), lambda b, h, i: (b, h)),
                        pl.BlockSpec((seq, width), lambda b, h, i: (b, h))],
        out_specs=pl.BlockSpec((t, width), lambda b, h, i: (b * nq + i, h)),
        out_shape=jax.ShapeDtypeStruct((batch * seq, H_DIFF * width), BF16),
        compiler_params=_params(("parallel", "parallel", "arbitrary")),
        name="diff_attn_prompt",
    )(*lam_vecs, subg, dq, dkb, dvb)


def _attn_decode_body(pt_ref, lq1_ref, lk1_ref, lq2_ref, lk2_ref, subg_ref, q_ref, knew_ref, vnew_ref, *refs,
                      pages, lam_init):
    k_refs, v_refs = refs[:pages], refs[pages:2 * pages]
    o_ref, probs, acc, p_new = refs[2 * pages:]
    phase, j = pl.program_id(1), pl.program_id(2)
    page_rows = k_refs[0].shape[0]
    q8 = q_ref[0]

    @pl.when(phase == 0)
    def _():
        for i in range(pages):
            kb = k_refs[i][...].astype(BF16)
            s = lax.dot_general(q8, kb, (((1,), (1,)), ((), ())), preferred_element_type=F32)
            off = pl.multiple_of((j * pages + i) * page_rows, page_rows)
            probs[:, pl.ds(off, page_rows)] = s

    @pl.when((phase == 1) & (j == 0))
    def _():
        s = probs[...]
        lane = lax.broadcasted_iota(jnp.int32, s.shape, 1)
        row = lax.broadcasted_iota(jnp.int32, s.shape, 0)
        s = jnp.where((lane & (H_DIFF - 1)) == (row & (H_DIFF - 1)), s, NEG_BIG)
        s_new = jnp.sum(q8.astype(F32) * knew_ref[0], axis=-1, keepdims=True)
        mx = jnp.maximum(jnp.max(s, axis=-1, keepdims=True), s_new)
        e = jnp.exp(s - mx)
        e_new = jnp.exp(s_new - mx)
        inv = 1.0 / (jnp.sum(e, axis=-1, keepdims=True) + e_new)
        probs[...] = e * inv
        p_new[...] = jnp.broadcast_to(e_new * inv, p_new.shape)
        acc[...] = jnp.zeros_like(acc)

    @pl.when(phase == 1)
    def _():
        total = acc[...]
        for i in range(pages):
            vb = v_refs[i][...].astype(BF16)
            off = pl.multiple_of((j * pages + i) * page_rows, page_rows)
            total += jnp.dot(probs[:, pl.ds(off, page_rows)].astype(BF16), vb, preferred_element_type=F32)
        acc[...] = total

    @pl.when((phase == 1) & (j == pl.num_programs(2) - 1))
    def _():
        a = acc[...] + p_new[...] * jnp.concatenate([vnew_ref[0], vnew_ref[0]], axis=0)
        lam = _lambda(lq1_ref, lk1_ref, lq2_ref, lk2_ref, lam_init)
        o = a[:H_DIFF] - lam * a[H_DIFF:]
        o_ref[0] = _sub_norm(o, subg_ref[...], lam_init).astype(BF16)


def _attn_decode(page_table, cache_k, cache_v, q8, knew8, vnew, lam_vecs, subg, lam_init):
    nb, n_pages = page_table.shape
    _, page_rows, width = cache_k.shape
    pages = DEC_PAGES if n_pages % DEC_PAGES == 0 else 1
    groups = n_pages // pages

    def k_map(i):
        return lambda b, ph, j, pt: (pt[b, jnp.where(ph == 0, j, groups - 1) * pages + i], 0, 0)

    def v_map(i):
        return lambda b, ph, j, pt: (pt[b, jnp.where(ph == 0, 0, j) * pages + i], 0, 0)

    page = lambda fn: pl.BlockSpec((None, page_rows, width), fn)
    per_sample = lambda a: pl.BlockSpec((1,) + a.shape[1:], lambda b, ph, j, pt: (b, 0, 0))
    small = lambda a: pl.BlockSpec(a.shape, lambda b, ph, j, pt: (0,) * a.ndim)
    grid_spec = pltpu.PrefetchScalarGridSpec(
        num_scalar_prefetch=1,
        grid=(nb, 2, groups),
        in_specs=[small(v) for v in lam_vecs] + [small(subg), per_sample(q8), per_sample(knew8), per_sample(vnew)]
        + [page(k_map(i)) for i in range(pages)] + [page(v_map(i)) for i in range(pages)],
        out_specs=pl.BlockSpec((1, H_DIFF, width), lambda b, ph, j, pt: (b, 0, 0)),
        scratch_shapes=[pltpu.VMEM((2 * H_DIFF, n_pages * page_rows), F32),
                        pltpu.VMEM((2 * H_DIFF, width), F32),
                        pltpu.VMEM((2 * H_DIFF, width), F32)],
    )
    return pl.pallas_call(
        functools.partial(_attn_decode_body, pages=pages, lam_init=lam_init),
        grid_spec=grid_spec,
        out_shape=jax.ShapeDtypeStruct((nb, H_DIFF, width), BF16),
        compiler_params=_params(("arbitrary", "arbitrary", "arbitrary")),
        name="diff_attn_decode",
    )(page_table, *lam_vecs, subg, q8, knew8, vnew, *([cache_k] * pages), *([cache_v] * pages))


def _mix_ffn_body(*refs, widths):
    n = len(widths)
    x_ref = refs[0]
    a_refs = refs[1:1 + n]
    wo_ref, g_ref, wup_ref, wdn_ref, y_ref = refs[1 + n:]
    a = a_refs[0][...] if n == 1 else jnp.concatenate([r[...] for r in a_refs], axis=-1)
    y = x_ref[...] + jnp.dot(a, wo_ref[...], preferred_element_type=F32)
    hb =_rms(y, g_ref[...]).astype(BF16)
    d_ff = wup_ref.shape[1]
    fc = FF_CHUNK if d_ff % FF_CHUNK == 0 else d_ff
    for c in range(d_ff // fc):
        hid = jnp.dot(hb, wup_ref[:, c * fc:(c + 1) * fc], preferred_element_type=F32)
        act = jnp.square(jnp.maximum(hid, 0.0)).astype(BF16)
        y = y + jnp.dot(act, wdn_ref[c * fc:(c + 1) * fc, :], preferred_element_type=F32)
    y_ref[...] = y


def _mix_ffn(x, parts, w_o, g, w_up, w_dn, tm):
    rows, d = x.shape
    widths = tuple(p.shape[1] for p in parts)
    row = lambda width: pl.BlockSpec((tm, width), lambda i: (i, 0))
    return pl.pallas_call(
        functools.partial(_mix_ffn_body, widths=widths),
        grid=(rows // tm,),
        in_specs=[row(d)] + [row(w) for w in widths]
        + [_resident(w_o.shape), _resident(g.shape), _resident(w_up.shape), _resident(w_dn.shape)],
        out_specs=row(d),
        out_shape=jax.ShapeDtypeStruct((rows, d), F32),
        compiler_params=_params(("parallel",)),
        name="mix_ffn",
    )(x, *parts, w_o, g, w_up, w_dn)


def _gelu(x):
    return 0.5 * x * (1.0 + lax.erf(x * (2.0 ** -0.5)))


def _gmlp_body(x_ref, g_ref, w_ref, lng_ref, lnb_ref, ws_ref, bs_ref, a_ref, *rest, e, single_token):
    if single_token:
        vn_ref, vbuf = rest
    else:
        (vbuf,) = rest
    hb = _rms(x_ref[...], g_ref[...]).astype(BF16)
    rows = hb.shape[0]
    gc = e // GMLP_GROUPS
    for grp in range(GMLP_GROUPS):
        cols = slice(grp * gc, (grp + 1) * gc)
        vbuf[:, cols] = _gelu(jnp.dot(hb, w_ref[:, e + grp * gc:e + (grp + 1) * gc], preferred_element_type=F32))
    v = vbuf[...]
    vc = v - jnp.mean(v, axis=-1, keepdims=True)
    vn = vc * lax.rsqrt(jnp.mean(vc * vc, axis=-1, keepdims=True) + EPS) * lng_ref[...] + lnb_ref[...]
    vbuf[...] = vn
    if single_token:
        vn_ref[...] = vn
    else:
        t = lax.broadcasted_iota(jnp.int32, (GMLP_CHUNK, GMLP_CHUNK), 0)
        s_ = lax.broadcasted_iota(jnp.int32, (GMLP_CHUNK, GMLP_CHUNK), 1)
        tril = t >= s_
    for grp in range(GMLP_GROUPS):
        cols = slice(grp * gc, (grp + 1) * gc)
        u = _gelu(jnp.dot(hb, w_ref[:, cols], preferred_element_type=F32))
        if single_token:
            a_ref[:, cols] = (u * (vbuf[:, cols] * ws_ref[grp][0:1, 0:1] + bs_ref[0:1, grp:grp + 1])).astype(BF16)
        else:
            wsg = jnp.where(tril, ws_ref[grp], 0.0).astype(BF16)
            bias = bs_ref[:, grp:grp + 1]
            for r in range(rows // GMLP_CHUNK):
                rsl = slice(r * GMLP_CHUNK, (r + 1) * GMLP_CHUNK)
                s = jnp.dot(wsg, vbuf[rsl, cols].astype(BF16), preferred_element_type=F32) + bias
                a_ref[rsl, cols] = (u[rsl] * s).astype(BF16)


def _gmlp(x, g, w, lng, lnb, ws, bs_t, tm, single_token):
    rows, d = x.shape
    e = w.shape[1] // 2
    row = lambda width: pl.BlockSpec((tm, width), lambda i: (i, 0))
    out_specs = [row(e)]
    out_shape = [jax.ShapeDtypeStruct((rows, e), BF16)]
    if single_token:
        out_specs.append(row(e))
        out_shape.append(jax.ShapeDtypeStruct((rows, e), F32))
    return pl.pallas_call(
        functools.partial(_gmlp_body, e=e, single_token=single_token),
        grid=(rows // tm,),
        in_specs=[row(d), _resident(g.shape), _resident(w.shape), _resident(lng.shape), _resident(lnb.shape),
                  _resident(ws.shape), _resident(bs_t.shape)],
        out_specs=out_specs,
        out_shape=out_shape,
        scratch_shapes=[pltpu.VMEM((tm, e), F32)],
        compiler_params=_params(("parallel",)),
        name="gmlp_sample" if single_token else "gmlp_prompt",
    )(x, g, w, lng, lnb, ws, bs_t)


def _rope_tables(pos, d):
    inv = 1.0 / (ROPE_THETA ** (jnp.arange(0, d, 2, dtype=F32) / d))
    ang = pos.astype(F32)[:, None] * inv[None, :]
    c, s = jnp.cos(ang), jnp.sin(ang)
    reps = LANES // d
    return jnp.tile(jnp.concatenate([c, c], -1), (1, reps)), jnp.tile(jnp.concatenate([-s, s], -1), (1, reps))


def kernel(x_prompt, x_sample, state_ret, cache_k, cache_v, page_table, rms_mix_g, rms_ffn_g, w_ffn_up, w_ffn_down,
           w_in_even, w_out_even, q_norm_g, k_norm_g, lam_q1, lam_k1, lam_q2, lam_k2, subln_g,
           w_in_odd, ln_v_g, ln_v_b, w_s, b_s, w_out_odd):
    bp, sp, d = x_prompt.shape
    bs, ls, _ = x_sample.shape
    depth = rms_mix_g.shape[0]
    dk_ret, dv_ret = state_ret.shape[-2:]
    n_pool, page_size = cache_k.shape[1:3]
    dh = cache_k.shape[-1] // 2
    dv_diff = cache_v.shape[-1]
    assert dk_ret == LANES and 2 * dh == LANES and dv_diff == LANES and ls == 1
    ret_qk, ret_v = H_RET * dk_ret, H_RET * dv_ret
    diff_qk, diff_v = H_DIFF * 2 * dh, H_DIFF * dv_diff
    dims = (ret_qk, ret_v, diff_qk, diff_v, dh)
    past = page_table.shape[1] * page_size
    tm = ROW_TILE if (bp * sp) % ROW_TILE == 0 and sp % ROW_TILE == 0 else sp
    rows_s = bs * ls

    pos_p = jnp.arange(sp, dtype=jnp.int32)
    pos_s = jnp.tile(past + jnp.arange(ls, dtype=jnp.int32), bs)
    tabs_p = _rope_tables(pos_p, dk_ret) + _rope_tables(pos_p, dh)
    tabs_s = _rope_tables(pos_s, dk_ret) + _rope_tables(pos_s, dh)
    gidx = jnp.arange(2 * LANES) // dh
    grp = (gidx[:, None] == gidx[None, :]).astype(BF16)

    yp = x_prompt.reshape(bp * sp, d)
    ys = x_sample.reshape(rows_s, d)
    ret_p, ret_s, k_p, v_p, k_s, v_s, gv_s = [], [], [], [], [], [], []
    for l in range(depth):
        g_mix = rms_mix_g[l][None]
        g_ffn = rms_ffn_g[l][None]
        w_up = w_ffn_up[l].astype(BF16)
        w_dn = w_ffn_down[l].astype(BF16)
        if l % 2 == 0:
            e = l // 2
            lam_init = 0.8 - 0.6 * math.exp(-0.3 * l)
            w_in = w_in_even[e].astype(BF16)
            w_out = w_out_even[e].astype(BF16)
            qg = jnp.tile(q_norm_g[e], LANES // dh)[None]
            kg = jnp.tile(k_norm_g[e], LANES // dh)[None]
            lam_vecs = [lam_q1[e][None], lam_k1[e][None], lam_q2[e][None], lam_k2[e][None]]
            subg = subln_g[e][None]

            rq, rk, rv, gate, dq, dk, dv, dkb, dvb = _in_proj(yp, g_mix, w_in, tabs_p, sp // tm, qg, kg, grp, tm, dims)
            ro, s_fin = _retention_prompt(rq, rk, rv, gate, bp, sp, dk_ret, dv_ret)
            do = _attn_prompt(dq, dkb, dvb, lam_vecs, subg, bp, sp, dh, lam_init)
            yp = _mix_ffn(yp, [ro, do], w_out, g_ffn, w_up, w_dn, tm)
            ret_p.append(s_fin)
            k_p.append(dk.reshape(bp, sp, H_DIFF, 2 * dh))
            v_p.append(dv.reshape(bp, sp, H_DIFF, dv_diff))

            rq, rk, rv, gate, dq, dk, dv, _, _ = _in_proj(ys, g_mix, w_in, tabs_s, 1, qg, kg, grp, rows_s, dims)
            s_new, ro = _retention_step(
                state_ret[e],
                rq.astype(F32).reshape(bs, H_RET, dk_ret, 1), rk.astype(F32).reshape(bs, H_RET, dk_ret, 1),
                rv.reshape(bs, H_RET, 1, dv_ret), gate.reshape(bs, H_RET, 1, dv_ret), 4 if bs % 4 == 0 else 1)
            q4 = dq.reshape(bs, H_DIFF, 2 * dh)
            half = (jnp.arange(2 * dh) < dh)[None, None, :]
            q8 = jnp.concatenate([jnp.where(half, q4, 0), jnp.where(half, 0, q4)], axis=1)
            k4 = dk.reshape(bs, H_DIFF, 2 * dh)
            do = _attn_decode(page_table,
                              cache_k[e].reshape(n_pool, page_size * H_DIFF, 2 * dh),
                              cache_v[e].reshape(n_pool, page_size * H_DIFF, dv_diff),
                              q8, jnp.concatenate([k4, k4], axis=1), dv.reshape(bs, H_DIFF, dv_diff),
                              lam_vecs, subg, lam_init)
            ys = _mix_ffn(ys, [ro.reshape(rows_s, ret_v), do.reshape(rows_s, diff_v)], w_out, g_ffn, w_up, w_dn, rows_s)
            ret_s.append(s_new)
            k_s.append(dk.reshape(bs, ls, H_DIFF, 2 * dh))
            v_s.append(dv.reshape(bs, ls, H_DIFF, dv_diff))
        else:
            o = l // 2
            w_in = w_in_odd[o].astype(BF16)
            w_out = w_out_odd[o].astype(BF16)
            lng, lnb = ln_v_g[o][None], ln_v_b[o][None]
            bs_t = b_s[o].T
            (a,) = _gmlp(yp, g_mix, w_in, lng, lnb, w_s[o], bs_t, min(tm, GMLP_TILE), False)
            yp = _mix_ffn(yp, [a], w_out, g_ffn, w_up, w_dn, tm)
            a, gv = _gmlp(ys, g_mix, w_in, lng, lnb, w_s[o], bs_t, rows_s, True)
            ys = _mix_ffn(ys, [a], w_out, g_ffn, w_up, w_dn, rows_s)
            gv_s.append(gv.reshape(bs, ls, -1))
    return (yp.reshape(bp, sp, d), ys.reshape(bs, ls, d), jnp.stack(ret_p), jnp.stack(ret_s),
            jnp.stack(k_p), jnp.stack(v_p), jnp.stack(k_s), jnp.stack(v_s), jnp.stack(gv_s))
```

```python
import functools
import math

import jax
import jax.numpy as jnp
from jax import lax
from jax.experimental import pallas as pl
from jax.experimental.pallas import tpu as pltpu

F32 = jnp.float32
BF16 = jnp.bfloat16
EPS = 1e-6
ROPE_THETA = 10000.0
H_RET = 4
H_DIFF = 4
GMLP_GROUPS = 8
GMLP_CHUNK = 128
LANES = 128
VMEM_LIMIT_BYTES = 56 * 1024 * 1024
NEG_BIG = -1e30

ROW_TILE = 512
GMLP_TILE = 256
RET_CHUNK = 256
ATTN_TILE = 512
FF_CHUNK = 512
DEC_PAGES = 32


def _params(semantics):
    return pltpu.CompilerParams(dimension_semantics=semantics, vmem_limit_bytes=VMEM_LIMIT_BYTES)


def _resident(shape):
    return pl.BlockSpec(shape, lambda *_: (0,) * len(shape), pipeline_mode=pl.Buffered(1))


def _rms(x, g):
    return x * lax.rsqrt(jnp.mean(x * x, axis=-1, keepdims=True) + EPS) * g


def _log_gamma(h):
    return math.log1p(-(2.0 ** (-5.0 - h)))


def _in_proj_body(x_ref, g_ref, w_ref, cr_ref, sr_ref, cd_ref, sd_ref, qg_ref, kg_ref, grp_ref,
                  rq_ref, rk_ref, rv_ref, gate_ref, dq_ref, dk_ref, dv_ref, dkb_ref, dvb_ref,
                  *, ret_qk, ret_v, diff_qk, diff_v, dh):
    hb = _rms(x_ref[...], g_ref[...]).astype(BF16)
    rows = hb.shape[0]

    def proj(lo, width):
        return jnp.dot(hb, w_ref[:, lo:lo + width], preferred_element_type=F32)

    cr, sr = cr_ref[...], sr_ref[...]
    for col0, out_ref in ((0, rq_ref), (ret_qk, rk_ref)):
        p = proj(col0, ret_qk)
        for c in range(ret_qk // LANES):
            s = p[:, c * LANES:(c + 1) * LANES]
            out_ref[:, c * LANES:(c + 1) * LANES] = (s * cr + pltpu.roll(s, LANES // 2, 1) * sr).astype(BF16)
    rv_ref[...] = proj(2 * ret_qk, ret_v).astype(BF16)
    rg = proj(2 * ret_qk + ret_v, ret_v)
    gate_ref[...] = (rg * jax.nn.sigmoid(rg)).astype(BF16)

    cd, sd = cd_ref[...], sd_ref[...]
    lane = lax.broadcasted_iota(jnp.int32, (rows, LANES), 1)
    first_half = (lane & (dh // 2)) == 0
    grp = grp_ref[...]
    gw = grp.shape[0]
    base = 2 * ret_qk + 2 * ret_v

    def norm_rope(col0, gain):
        p = proj(col0, diff_qk)
        slabs = []
        for c in range(diff_qk // gw):
            s = p[:, c * gw:(c + 1) * gw]
            ssq = jnp.dot((s * s).astype(BF16), grp, preferred_element_type=F32)
            n = s * lax.rsqrt(ssq * (1.0 / dh) + EPS)
            for d in range(gw // LANES):
                t = n[:, d * LANES:(d + 1) * LANES] * gain
                partner = jnp.where(first_half, pltpu.roll(t, LANES - dh // 2, 1), pltpu.roll(t, dh // 2, 1))
                slabs.append(t * cd + partner * sd)
        return slabs

    q_scale = dh ** -0.5
    for c, s in enumerate(norm_rope(base, qg_ref[...])):
        dq_ref[:, c * LANES:(c + 1) * LANES] = (s * q_scale).astype(BF16)
    for c, s in enumerate(norm_rope(base + diff_qk, kg_ref[...])):
        dk_ref[:, c * LANES:(c + 1) * LANES] = s
        dkb_ref[:, c * LANES:(c + 1) * LANES] = s.astype(BF16)
    dv = proj(base + 2 * diff_qk, diff_v)
    dv_ref[...] = dv
    dvb_ref[...] = dv.astype(BF16)


def _in_proj(x, g, w, tabs, tab_blocks, qg, kg, grp, tm, dims):
    rows, d = x.shape
    ret_qk, ret_v, diff_qk, diff_v, dh = dims
    n = rows // tm
    row = lambda width: pl.BlockSpec((tm, width), lambda i: (i, 0))
    tab = pl.BlockSpec((tm, LANES), lambda i: (i % tab_blocks, 0))
    body = functools.partial(_in_proj_body, ret_qk=ret_qk, ret_v=ret_v, diff_qk=diff_qk, diff_v=diff_v, dh=dh)
    shapes = [((rows, ret_qk), BF16), ((rows, ret_qk), BF16), ((rows, ret_v), BF16), ((rows, ret_v), BF16),
              ((rows, diff_qk), BF16), ((rows, diff_qk), F32), ((rows, diff_v), F32),
              ((rows, diff_qk), BF16), ((rows, diff_v), BF16)]
    return pl.pallas_call(
        body,
        grid=(n,),
        in_specs=[row(d), _resident(g.shape), _resident(w.shape), tab, tab, tab, tab,
                  _resident(qg.shape), _resident(kg.shape), _resident(grp.shape)],
        out_specs=[row(s[1]) for s, _ in shapes],
        out_shape=[jax.ShapeDtypeStruct(s, t) for s, t in shapes],
        compiler_params=_params(("parallel",)),
        name="in_proj",
    )(x, g, w, *tabs, qg, kg, grp)


def _retention_body(q_ref, k_ref, v_ref, gate_ref, ro_ref, sfin_ref, state, dmask, *, dk, dv):
    b, c = pl.program_id(0), pl.program_id(1)
    chunk = q_ref.shape[0]
    scale = dk ** -0.5

    @pl.when((b == 0) & (c == 0))
    def _():
        i = lax.broadcasted_iota(jnp.int32, (chunk, chunk), 0)
        j = lax.broadcasted_iota(jnp.int32, (chunk, chunk), 1)
        causal = i >= j
        diff = jnp.where(causal, i - j, 0).astype(F32)
        for h in range(H_RET):
            dmask[h] = jnp.where(causal, jnp.exp(diff * _log_gamma(h)), 0.0) * scale

    @pl.when(c == 0)
    def _():
        state[...] = jnp.zeros_like(state)

    idx = lax.broadcasted_iota(jnp.int32, (chunk, 1), 0).astype(F32)
    for h in range(H_RET):
        lg = _log_gamma(h)
        q = q_ref[:, h * dk:(h + 1) * dk]
        k = k_ref[:, h * dk:(h + 1) * dk]
        v = v_ref[:, h * dv:(h + 1) * dv]
        s_prev = state[h]
        scores = lax.dot_general(q, k, (((1,), (1,)), ((), ())), preferred_element_type=F32) * dmask[h]
        inner = jnp.dot(scores.astype(BF16), v, preferred_element_type=F32)
        cross = jnp.dot(q, s_prev.astype(BF16), preferred_element_type=F32) * jnp.exp((idx + 1.0) * lg)
        o = inner + cross
        zeta = jnp.exp((chunk - 1.0 - idx) * lg) * scale
        kz_t = (k.astype(F32) * zeta).T.astype(BF16)
        state[h] = math.exp(chunk * lg) * s_prev + jnp.dot(kz_t, v, preferred_element_type=F32)
        oc = o - jnp.mean(o, axis=-1, keepdims=True)
        on = oc * lax.rsqrt(jnp.mean(oc * oc, axis=-1, keepdims=True) + EPS)
        ro_ref[:, h * dv:(h + 1) * dv] = (on * gate_ref[:, h * dv:(h + 1) * dv].astype(F32)).astype(BF16)

    @pl.when(c == pl.num_programs(1) - 1)
    def _():
        sfin_ref[0] = state[...]


def _retention_prompt(rq, rk, rv, gate, batch, seq, dk, dv):
    chunk = RET_CHUNK if seq % RET_CHUNK == 0 else seq
    nc = seq // chunk
    blk = lambda width: pl.BlockSpec((chunk, width), lambda b, c: (b * nc + c, 0))
    return pl.pallas_call(
        functools.partial(_retention_body, dk=dk, dv=dv),
        grid=(batch, nc),
        in_specs=[blk(H_RET * dk), blk(H_RET * dk), blk(H_RET * dv), blk(H_RET * dv)],
        out_specs=[blk(H_RET * dv), pl.BlockSpec((1, H_RET, dk, dv), lambda b, c: (b, 0, 0, 0))],
        out_shape=[jax.ShapeDtypeStruct((batch * seq, H_RET * dv), BF16),
                   jax.ShapeDtypeStruct((batch, H_RET, dk, dv), F32)],
        scratch_shapes=[pltpu.VMEM((H_RET, dk, dv), F32), pltpu.VMEM((H_RET, chunk, chunk), F32)],
        compiler_params=_params(("arbitrary", "arbitrary")),
        name="retention_prompt",
    )(rq, rk, rv, gate)


def _retention_step_body(s_ref, q_ref, k_ref, v_ref, gate_ref, snew_ref, ro_ref, *, dk):
    scale = dk ** -0.5
    for h in range(H_RET):
        s_new = math.exp(_log_gamma(h)) * s_ref[:, h] + (k_ref[:, h] * scale) * v_ref[:, h]
        snew_ref[:, h] = s_new
        o = jnp.sum(q_ref[:, h] * s_new, axis=1, keepdims=True)
        oc = o - jnp.mean(o, axis=-1, keepdims=True)
        on = oc * lax.rsqrt(jnp.mean(oc * oc, axis=-1, keepdims=True) + EPS)
        ro_ref[:, h] = (on * gate_ref[:, h].astype(F32)).astype(BF16)


def _retention_step(state, q_col, k_col, v_row, gate_row, bt):
    nb, _, dk, dv = state.shape
    col = pl.BlockSpec((bt, H_RET, dk, 1), lambda i: (i, 0, 0, 0))
    rowb = pl.BlockSpec((bt, H_RET, 1, dv), lambda i: (i, 0, 0, 0))
    st = pl.BlockSpec((bt, H_RET, dk, dv), lambda i: (i, 0, 0, 0))
    return pl.pallas_call(
        functools.partial(_retention_step_body, dk=dk),
        grid=(nb // bt,),
        in_specs=[st, col, col, rowb, rowb],
        out_specs=[st, rowb],
        out_shape=[jax.ShapeDtypeStruct(state.shape, F32), jax.ShapeDtypeStruct((nb, H_RET, 1, dv), BF16)],
        compiler_params=_params(("parallel",)),
        name="retention_step",
    )(state, q_col, k_col, v_row, gate_row)


def _lambda(lq1_ref, lk1_ref, lq2_ref, lk2_ref, lam_init):
    a = jnp.sum(lq1_ref[...] * lk1_ref[...], axis=-1, keepdims=True)
    b = jnp.sum(lq2_ref[...] * lk2_ref[...], axis=-1, keepdims=True)
    return jnp.exp(a) - jnp.exp(b) + lam_init


def _sub_norm(o, subg, lam_init):
    return o * lax.rsqrt(jnp.mean(o * o, axis=-1, keepdims=True) + EPS) * subg * (1.0 - lam_init)


def _attn_prompt_body(lq1_ref, lk1_ref, lq2_ref, lk2_ref, subg_ref, q_ref, k_ref, v_ref, o_ref, acc_ref,
                      *, dh, lam_init):
    i = pl.program_id(2)
    t = q_ref.shape[0]
    dv = v_ref.shape[1]
    q = q_ref[...]
    lane = lax.broadcasted_iota(jnp.int32, q.shape, 1)
    zero = jnp.zeros_like(q)
    q2 = jnp.concatenate([jnp.where(lane < dh, q, zero), jnp.where(lane >= dh, q, zero)], axis=0)
    row = lax.broadcasted_iota(jnp.int32, (2 * t, t), 0)
    col = lax.broadcasted_iota(jnp.int32, (2 * t, t), 1)
    on_or_below = col <= jnp.where(row >= t, row - t, row)
    ones = jnp.ones((t, LANES), BF16)

    def scores(j):
        kb = k_ref[pl.ds(pl.multiple_of(j * t, t), t), :]
        return lax.dot_general(q2, kb, (((1,), (1,)), ((), ())), preferred_element_type=F32)

    def lane_max(s, m):
        for c in range(t // LANES):
            m = jnp.maximum(m, s[:, c * LANES:(c + 1) * LANES])
        return m

    def accumulate(j, s, mrow):
        p = jnp.exp(s - mrow).astype(BF16)
        v1 = jnp.concatenate([v_ref[pl.ds(pl.multiple_of(j * t, t), t), :], ones], axis=1)
        acc_ref[...] += jnp.dot(p, v1, preferred_element_type=F32)

    s_diag = jnp.where(on_or_below, scores(i), NEG_BIG)
    m = lax.fori_loop(0, i, lambda j, m: lane_max(scores(j), m),
                      lane_max(s_diag, jnp.full((2 * t, LANES), NEG_BIG, F32)))
    mrow = jnp.max(m, axis=-1, keepdims=True)
    acc_ref[...] = jnp.zeros_like(acc_ref)
    accumulate(i, s_diag, mrow)

    def body(j, carry):
        accumulate(j, scores(j), mrow)
        return carry

    lax.fori_loop(0, i, body, 0)
    acc = acc_ref[...]
    o_all = acc[:, :dv] / acc[:, dv:]
    lam = _lambda(lq1_ref, lk1_ref, lq2_ref, lk2_ref, lam_init)
    o = o_all[:t] - lam * o_all[t:]
    o_ref[...] = _sub_norm(o, subg_ref[...], lam_init).astype(BF16)


def _attn_prompt(dq, dkb, dvb, lam_vecs, subg, batch, seq, dh, lam_init):
    t = ATTN_TILE if seq % ATTN_TILE == 0 else seq
    nq = seq // t
    width = 2 * dh
    vec = [_resident(v.shape) for v in lam_vecs]
    return pl.pallas_call(
        functools.partial(_attn_prompt_body, dh=dh, lam_init=lam_init),
        grid=(batch, H_DIFF, nq),
        in_specs=vec + [_resident(subg.shape),
                        pl.BlockSpec((t, width), lambda b, h, i: (b * nq + i, h)),
                        pl.BlockSpec((seq, width), lambda b, h, i: (b, h)),
                        pl.BlockSpec((seq, width), lambda b, h, i: (b, h))],
        out_specs=pl.BlockSpec((t, width), lambda b, h, i: (b * nq + i, h)),
        out_shape=jax.ShapeDtypeStruct((batch * seq, H_DIFF * width), BF16),
        scratch_shapes=[pltpu.VMEM((2 * t, 2 * width), F32)],
        compiler_params=_params(("parallel", "parallel", "arbitrary")),
        name="diff_attn_prompt",
    )(*lam_vecs, subg, dq, dkb, dvb)


def _attn_decode_body(pt_ref, lq1_ref, lk1_ref, lq2_ref, lk2_ref, subg_ref, q_ref, knew_ref, vnew_ref, *refs,
                      pages, lam_init):
    k_refs, v_refs = refs[:pages], refs[pages:2 * pages]
    o_ref, probs, acc, p_new = refs[2 * pages:]
    phase, j = pl.program_id(1), pl.program_id(2)
    page_rows = k_refs[0].shape[0]
    q8 = q_ref[0]

    @pl.when(phase == 0)
    def _():
        for i in range(pages):
            kb = k_refs[i][...].astype(BF16)
            s = lax.dot_general(q8, kb, (((1,), (1,)), ((), ())), preferred_element_type=F32)
            off = pl.multiple_of((j * pages + i) * page_rows, page_rows)
            probs[:, pl.ds(off, page_rows)] = s

    @pl.when((phase == 1) & (j == 0))
    def _():
        s = probs[...]
        lane = lax.broadcasted_iota(jnp.int32, s.shape, 1)
        row = lax.broadcasted_iota(jnp.int32, s.shape, 0)
        s = jnp.where((lane & (H_DIFF - 1)) == (row & (H_DIFF - 1)), s, NEG_BIG)
        s_new = jnp.sum(q8.astype(F32) * knew_ref[0], axis=-1, keepdims=True)
        mx = jnp.maximum(jnp.max(s, axis=-1, keepdims=True), s_new)
        e = jnp.exp(s - mx)
        e_new = jnp.exp(s_new - mx)
        inv = 1.0 / (jnp.sum(e, axis=-1, keepdims=True) + e_new)
        probs[...] = e * inv
        p_new[...] = jnp.broadcast_to(e_new * inv, p_new.shape)
        acc[...] = jnp.zeros_like(acc)

    @pl.when(phase == 1)
    def _():
        total = acc[...]
        for i in range(pages):
            vb = v_refs[i][...].astype(BF16)
            off = pl.multiple_of((j * pages + i) * page_rows, page_rows)
            total += jnp.dot(probs[:, pl.ds(off, page_rows)].astype(BF16), vb, preferred_element_type=F32)
        acc[...] = total

    @pl.when((phase == 1) & (j == pl.num_programs(2) - 1))
    def _():
        a = acc[...] + p_new[...] * jnp.concatenate([vnew_ref[0], vnew_ref[0]], axis=0)
        lam = _lambda(lq1_ref, lk1_ref, lq2_ref, lk2_ref, lam_init)
        o = a[:H_DIFF] - lam * a[H_DIFF:]
        o_ref[0] = _sub_norm(o, subg_ref[...], lam_init).astype(BF16)


def _attn_decode(page_table, cache_k, cache_v, q8, knew8, vnew, lam_vecs, subg, lam_init):
    nb, n_pages = page_table.shape
    _, page_rows, width = cache_k.shape
    pages = DEC_PAGES if n_pages % DEC_PAGES == 0 else 1
    groups = n_pages // pages

    def k_map(i):
        return lambda b, ph, j, pt: (pt[b, jnp.where(ph == 0, j, groups - 1) * pages + i], 0, 0)

    def v_map(i):
        return lambda b, ph, j, pt: (pt[b, jnp.where(ph == 0, 0, j) * pages + i], 0, 0)

    page = lambda fn: pl.BlockSpec((None, page_rows, width), fn)
    per_sample = lambda a: pl.BlockSpec((1,) + a.shape[1:], lambda b, ph, j, pt: (b, 0, 0))
    small = lambda a: pl.BlockSpec(a.shape, lambda b, ph, j, pt: (0,) * a.ndim)
    grid_spec = pltpu.PrefetchScalarGridSpec(
        num_scalar_prefetch=1,
        grid=(nb, 2, groups),
        in_specs=[small(v) for v in lam_vecs] + [small(subg), per_sample(q8), per_sample(knew8), per_sample(vnew)]
        + [page(k_map(i)) for i in range(pages)] + [page(v_map(i)) for i in range(pages)],
        out_specs=pl.BlockSpec((1, H_DIFF, width), lambda b, ph, j, pt: (b, 0, 0)),
        scratch_shapes=[pltpu.VMEM((2 * H_DIFF, n_pages * page_rows), F32),
                        pltpu.VMEM((2 * H_DIFF, width), F32),
                        pltpu.VMEM((2 * H_DIFF, width), F32)],
    )
    return pl.pallas_call(
        functools.partial(_attn_decode_body, pages=pages, lam_init=lam_init),
        grid_spec=grid_spec,
        out_shape=jax.ShapeDtypeStruct((nb, H_DIFF, width), BF16),
        compiler_params=_params(("arbitrary", "arbitrary", "arbitrary")),
        name="diff_attn_decode",
    )(page_table, *lam_vecs, subg, q8, knew8, vnew, *([cache_k] * pages), *([cache_v] * pages))


def _mix_ffn_body(*refs, widths):
    n = len(widths)
    x_ref = refs[0]
    a_refs = refs[1:1 + n]
    wo_ref, g_ref, wup_ref, wdn_ref, y_ref = refs[1 + n:]
    a = a_refs[0][...] if n == 1 else jnp.concatenate([r[...] for r in a_refs], axis=-1)
    y = x_ref[...] + jnp.dot(a, wo_ref[...], preferred_element_type=F32)
    hb =_rms(y, g_ref[...]).astype(BF16)
    d_ff = wup_ref.shape[1]
    fc = FF_CHUNK if d_ff % FF_CHUNK == 0 else d_ff
    for c in range(d_ff // fc):
        hid = jnp.dot(hb, wup_ref[:, c * fc:(c + 1) * fc], preferred_element_type=F32)
        act = jnp.square(jnp.maximum(hid, 0.0)).astype(BF16)
        y = y + jnp.dot(act, wdn_ref[c * fc:(c + 1) * fc, :], preferred_element_type=F32)
    y_ref[...] = y


def _mix_ffn(x, parts, w_o, g, w_up, w_dn, tm):
    rows, d = x.shape
    widths = tuple(p.shape[1] for p in parts)
    row = lambda width: pl.BlockSpec((tm, width), lambda i: (i, 0))
    return pl.pallas_call(
        functools.partial(_mix_ffn_body, widths=widths),
        grid=(rows // tm,),
        in_specs=[row(d)] + [row(w) for w in widths]
        + [_resident(w_o.shape), _resident(g.shape), _resident(w_up.shape), _resident(w_dn.shape)],
        out_specs=row(d),
        out_shape=jax.ShapeDtypeStruct((rows, d), F32),
        compiler_params=_params(("parallel",)),
        name="mix_ffn",
    )(x, *parts, w_o, g, w_up, w_dn)


def _gelu(x):
    return 0.5 * x * (1.0 + lax.erf(x * (2.0 ** -0.5)))


def _gmlp_body(x_ref, g_ref, w_ref, lng_ref, lnb_ref, ws_ref, bs_ref, a_ref, *rest, e, single_token):
    if single_token:
        vn_ref, vbuf = rest
    else:
        (vbuf,) = rest
    hb = _rms(x_ref[...], g_ref[...]).astype(BF16)
    rows = hb.shape[0]
    gc = e // GMLP_GROUPS
    for grp in range(GMLP_GROUPS):
        cols = slice(grp * gc, (grp + 1) * gc)
        vbuf[:, cols] = _gelu(jnp.dot(hb, w_ref[:, e + grp * gc:e + (grp + 1) * gc], preferred_element_type=F32))
    v = vbuf[...]
    vc = v - jnp.mean(v, axis=-1, keepdims=True)
    vn = vc * lax.rsqrt(jnp.mean(vc * vc, axis=-1, keepdims=True) + EPS) * lng_ref[...] + lnb_ref[...]
    vbuf[...] = vn
    if single_token:
        vn_ref[...] = vn
    else:
        t = lax.broadcasted_iota(jnp.int32, (GMLP_CHUNK, GMLP_CHUNK), 0)
        s_ = lax.broadcasted_iota(jnp.int32, (GMLP_CHUNK, GMLP_CHUNK), 1)
        tril = t >= s_
    for grp in range(GMLP_GROUPS):
        cols = slice(grp * gc, (grp + 1) * gc)
        u = _gelu(jnp.dot(hb, w_ref[:, cols], preferred_element_type=F32))
        if single_token:
            a_ref[:, cols] = (u * (vbuf[:, cols] * ws_ref[grp][0:1, 0:1] + bs_ref[0:1, grp:grp + 1])).astype(BF16)
        else:
            wsg = jnp.where(tril, ws_ref[grp], 0.0).astype(BF16)
            bias = bs_ref[:, grp:grp + 1]
            for r in range(rows // GMLP_CHUNK):
                rsl = slice(r * GMLP_CHUNK, (r + 1) * GMLP_CHUNK)
                s = jnp.dot(wsg, vbuf[rsl, cols].astype(BF16), preferred_element_type=F32) + bias
                a_ref[rsl, cols] = (u[rsl] * s).astype(BF16)


def _gmlp(x, g, w, lng, lnb, ws, bs_t, tm, single_token):
    rows, d = x.shape
    e = w.shape[1] // 2
    row = lambda width: pl.BlockSpec((tm, width), lambda i: (i, 0))
    out_specs = [row(e)]
    out_shape = [jax.ShapeDtypeStruct((rows, e), BF16)]
    if single_token:
        out_specs.append(row(e))
        out_shape.append(jax.ShapeDtypeStruct((rows, e), F32))
    return pl.pallas_call(
        functools.partial(_gmlp_body, e=e, single_token=single_token),
        grid=(rows // tm,),
        in_specs=[row(d), _resident(g.shape), _resident(w.shape), _resident(lng.shape), _resident(lnb.shape),
                  _resident(ws.shape), _resident(bs_t.shape)],
        out_specs=out_specs,
        out_shape=out_shape,
        scratch_shapes=[pltpu.VMEM((tm, e), F32)],
        compiler_params=_params(("parallel",)),
        name="gmlp_sample" if single_token else "gmlp_prompt",
    )(x, g, w, lng, lnb, ws, bs_t)


def _rope_tables(pos, d):
    inv = 1.0 / (ROPE_THETA ** (jnp.arange(0, d, 2, dtype=F32) / d))
    ang = pos.astype(F32)[:, None] * inv[None, :]
    c, s = jnp.cos(ang), jnp.sin(ang)
    reps = LANES // d
    return jnp.tile(jnp.concatenate([c, c], -1), (1, reps)), jnp.tile(jnp.concatenate([-s, s], -1), (1, reps))


def kernel(x_prompt, x_sample, state_ret, cache_k, cache_v, page_table, rms_mix_g, rms_ffn_g, w_ffn_up, w_ffn_down,
           w_in_even, w_out_even, q_norm_g, k_norm_g, lam_q1, lam_k1, lam_q2, lam_k2, subln_g,
           w_in_odd, ln_v_g, ln_v_b, w_s, b_s, w_out_odd):
    bp, sp, d = x_prompt.shape
    bs, ls, _ = x_sample.shape
    depth = rms_mix_g.shape[0]
    dk_ret, dv_ret = state_ret.shape[-2:]
    n_pool, page_size = cache_k.shape[1:3]
    dh = cache_k.shape[-1] // 2
    dv_diff = cache_v.shape[-1]
    assert dk_ret == LANES and 2 * dh == LANES and dv_diff == LANES and ls == 1
    ret_qk, ret_v = H_RET * dk_ret, H_RET * dv_ret
    diff_qk, diff_v = H_DIFF * 2 * dh, H_DIFF * dv_diff
    dims = (ret_qk, ret_v, diff_qk, diff_v, dh)
    past = page_table.shape[1] * page_size
    tm = ROW_TILE if (bp * sp) % ROW_TILE == 0 and sp % ROW_TILE == 0 else sp
    rows_s = bs * ls

    pos_p = jnp.arange(sp, dtype=jnp.int32)
    pos_s = jnp.tile(past + jnp.arange(ls, dtype=jnp.int32), bs)
    tabs_p = _rope_tables(pos_p, dk_ret) + _rope_tables(pos_p, dh)
    tabs_s = _rope_tables(pos_s, dk_ret) + _rope_tables(pos_s, dh)
    gidx = jnp.arange(2 * LANES) // dh
    grp = (gidx[:, None] == gidx[None, :]).astype(BF16)

    yp = x_prompt.reshape(bp * sp, d)
    ys = x_sample.reshape(rows_s, d)
    ret_p, ret_s, k_p, v_p, k_s, v_s, gv_s = [], [], [], [], [], [], []
    for l in range(depth):
        g_mix = rms_mix_g[l][None]
        g_ffn = rms_ffn_g[l][None]
        w_up = w_ffn_up[l].astype(BF16)
        w_dn = w_ffn_down[l].astype(BF16)
        if l % 2 == 0:
            e = l // 2
            lam_init = 0.8 - 0.6 * math.exp(-0.3 * l)
            w_in = w_in_even[e].astype(BF16)
            w_out = w_out_even[e].astype(BF16)
            qg = jnp.tile(q_norm_g[e], LANES // dh)[None]
            kg = jnp.tile(k_norm_g[e], LANES // dh)[None]
            lam_vecs = [lam_q1[e][None], lam_k1[e][None], lam_q2[e][None], lam_k2[e][None]]
            subg = subln_g[e][None]

            rq, rk, rv, gate, dq, dk, dv, dkb, dvb = _in_proj(yp, g_mix, w_in, tabs_p, sp // tm, qg, kg, grp, tm, dims)
            ro, s_fin = _retention_prompt(rq, rk, rv, gate, bp, sp, dk_ret, dv_ret)
            do = _attn_prompt(dq, dkb, dvb, lam_vecs, subg, bp, sp, dh, lam_init)
            yp = _mix_ffn(yp, [ro, do], w_out, g_ffn, w_up, w_dn, tm)
            ret_p.append(s_fin)
            k_p.append(dk.reshape(bp, sp, H_DIFF, 2 * dh))
            v_p.append(dv.reshape(bp, sp, H_DIFF, dv_diff))

            rq, rk, rv, gate, dq, dk, dv, _, _ = _in_proj(ys, g_mix, w_in, tabs_s, 1, qg, kg, grp, rows_s, dims)
            s_new, ro = _retention_step(
                state_ret[e],
                rq.astype(F32).reshape(bs, H_RET, dk_ret, 1), rk.astype(F32).reshape(bs, H_RET, dk_ret, 1),
                rv.reshape(bs, H_RET, 1, dv_ret), gate.reshape(bs, H_RET, 1, dv_ret), 4 if bs % 4 == 0 else 1)
            q4 = dq.reshape(bs, H_DIFF, 2 * dh)
            half = (jnp.arange(2 * dh) < dh)[None, None, :]
            q8 = jnp.concatenate([jnp.where(half, q4, 0), jnp.where(half, 0, q4)], axis=1)
            k4 = dk.reshape(bs, H_DIFF, 2 * dh)
            do = _attn_decode(page_table,
                              cache_k[e].reshape(n_pool, page_size * H_DIFF, 2 * dh),
                              cache_v[e].reshape(n_pool, page_size * H_DIFF, dv_diff),
                              q8, jnp.concatenate([k4, k4], axis=1), dv.reshape(bs, H_DIFF, dv_diff),
                              lam_vecs, subg, lam_init)
            ys = _mix_ffn(ys, [ro.reshape(rows_s, ret_v), do.reshape(rows_s, diff_v)], w_out, g_ffn, w_up, w_dn, rows_s)
            ret_s.append(s_new)
            k_s.append(dk.reshape(bs, ls, H_DIFF, 2 * dh))
            v_s.append(dv.reshape(bs, ls, H_DIFF, dv_diff))
        else:
            o = l // 2
            w_in = w_in_odd[o].astype(BF16)
            w_out = w_out_odd[o].astype(BF16)
            lng, lnb = ln_v_g[o][None], ln_v_b[o][None]
            bs_t = b_s[o].T
            (a,) = _gmlp(yp, g_mix, w_in, lng, lnb, w_s[o], bs_t, min(tm, GMLP_TILE), False)
            yp = _mix_ffn(yp, [a], w_out, g_ffn, w_up, w_dn, tm)
            a, gv = _gmlp(ys, g_mix, w_in, lng, lnb, w_s[o], bs_t, rows_s, True)
            ys = _mix_ffn(ys, [a], w_out, g_ffn, w_up, w_dn, rows_s)
            gv_s.append(gv.reshape(bs, ls, -1))
    return (yp.reshape(bp, sp, d), ys.reshape(bs, ls, d), jnp.stack(ret_p), jnp.stack(ret_s),
            jnp.stack(k_p), jnp.stack(v_p), jnp.stack(k_s), jnp.stack(v_s), jnp.stack(gv_s))
```

```python
import functools
import math
from typing import NamedTuple

import jax
import jax.numpy as jnp
from jax import lax
from jax.experimental import pallas as pl
from jax.experimental.pallas import tpu as pltpu

F32 = jnp.float32
BF16 = jnp.bfloat16
EPS = 1e-6
ROPE_THETA = 10000.0
H_RET = 4
H_DIFF = 4
GMLP_GROUPS = 8
GMLP_CHUNK = 128
LANES = 128
VMEM_LIMIT_BYTES = 58 * 1024 * 1024
NEG_BIG = -1e30

ROW_TILE = 512
GMLP_TILE = 256
RET_CHUNK = 256
ATTN_TILE = 512
FF_CHUNK = 512
DEC_CHUNK_PAGES = 8
DEC_SLOTS = 4
DEC_SOFTMAX_PIECE = 8192


def _params(semantics):
    return pltpu.CompilerParams(dimension_semantics=semantics, vmem_limit_bytes=VMEM_LIMIT_BYTES)


def _resident(shape):
    return pl.BlockSpec(shape, lambda *_: (0,) * len(shape), pipeline_mode=pl.Buffered(1))


def _rms(x, g):
    return x * lax.rsqrt(jnp.mean(x * x, axis=-1, keepdims=True) + EPS) * g


def _log_gamma(h):
    return math.log1p(-(2.0 ** (-5.0 - h)))


def _in_proj_body(x_ref, g_ref, w_ref, cr_ref, sr_ref, cd_ref, sd_ref, qg_ref, kg_ref, grp_ref,
                  rq_ref, rk_ref, rv_ref, gate_ref, dq_ref, dk_ref, dv_ref, dkb_ref, dvb_ref,
                  *, ret_qk, ret_v, diff_qk, diff_v, dh):
    hb = _rms(x_ref[...], g_ref[...]).astype(BF16)
    rows = hb.shape[0]

    def proj(lo, width):
        return jnp.dot(hb, w_ref[:, lo:lo + width], preferred_element_type=F32)

    cr, sr = cr_ref[...], sr_ref[...]
    for col0, out_ref in ((0, rq_ref), (ret_qk, rk_ref)):
        p = proj(col0, ret_qk)
        for c in range(ret_qk // LANES):
            s = p[:, c * LANES:(c + 1) * LANES]
            out_ref[:, c * LANES:(c + 1) * LANES] = (s * cr + pltpu.roll(s, LANES // 2, 1) * sr).astype(BF16)
    rv_ref[...] = proj(2 * ret_qk, ret_v).astype(BF16)
    rg = proj(2 * ret_qk + ret_v, ret_v)
    gate_ref[...] = (rg * jax.nn.sigmoid(rg)).astype(BF16)

    cd, sd = cd_ref[...], sd_ref[...]
    lane = lax.broadcasted_iota(jnp.int32, (rows, LANES), 1)
    first_half = (lane & (dh // 2)) == 0
    grp = grp_ref[...]
    gw = grp.shape[0]
    base = 2 * ret_qk + 2 * ret_v

    def norm_rope(col0, gain):
        p = proj(col0, diff_qk)
        slabs = []
        for c in range(diff_qk // gw):
            s = p[:, c * gw:(c + 1) * gw]
            ssq = jnp.dot((s * s).astype(BF16), grp, preferred_element_type=F32)
            n = s * lax.rsqrt(ssq * (1.0 / dh) + EPS)
            for d in range(gw // LANES):
                t = n[:, d * LANES:(d + 1) * LANES] * gain
                partner = jnp.where(first_half, pltpu.roll(t, LANES - dh // 2, 1), pltpu.roll(t, dh // 2, 1))
                slabs.append(t * cd + partner * sd)
        return slabs

    q_scale = dh ** -0.5
    for c, s in enumerate(norm_rope(base, qg_ref[...])):
        dq_ref[:, c * LANES:(c + 1) * LANES] = (s * q_scale).astype(BF16)
    for c, s in enumerate(norm_rope(base + diff_qk, kg_ref[...])):
        dk_ref[:, c * LANES:(c + 1) * LANES] = s
        dkb_ref[:, c * LANES:(c + 1) * LANES] = s.astype(BF16)
    dv = proj(base + 2 * diff_qk, diff_v)
    dv_ref[...] = dv
    dvb_ref[...] = dv.astype(BF16)


def _in_proj(x, g, w, tabs, tab_blocks, qg, kg, grp, tm, dims):
    rows, d = x.shape
    ret_qk, ret_v, diff_qk, diff_v, dh = dims
    n = rows // tm
    row = lambda width: pl.BlockSpec((tm, width), lambda i: (i, 0))
    tab = pl.BlockSpec((tm, LANES), lambda i: (i % tab_blocks, 0))
    body = functools.partial(_in_proj_body, ret_qk=ret_qk, ret_v=ret_v, diff_qk=diff_qk, diff_v=diff_v, dh=dh)
    shapes = [((rows, ret_qk), BF16), ((rows, ret_qk), BF16), ((rows, ret_v), BF16), ((rows, ret_v), BF16),
              ((rows, diff_qk), BF16), ((rows, diff_qk), F32), ((rows, diff_v), F32),
              ((rows, diff_qk), BF16), ((rows, diff_v), BF16)]
    return pl.pallas_call(
        body,
        grid=(n,),
        in_specs=[row(d), _resident(g.shape), _resident(w.shape), tab, tab, tab, tab,
                  _resident(qg.shape), _resident(kg.shape), _resident(grp.shape)],
        out_specs=[row(s[1]) for s, _ in shapes],
        out_shape=[jax.ShapeDtypeStruct(s, t) for s, t in shapes],
        compiler_params=_params(("parallel",)),
        name="in_proj",
    )(x, g, w, *tabs, qg, kg, grp)


def _retention_body(q_ref, k_ref, v_ref, gate_ref, ro_ref, sfin_ref, state, dmask, *, dk, dv):
    b, c = pl.program_id(0), pl.program_id(1)
    chunk = q_ref.shape[0]
    scale = dk ** -0.5

    @pl.when((b == 0) & (c == 0))
    def _():
        i = lax.broadcasted_iota(jnp.int32, (chunk, chunk), 0)
        j = lax.broadcasted_iota(jnp.int32, (chunk, chunk), 1)
        causal = i >= j
        diff = jnp.where(causal, i - j, 0).astype(F32)
        for h in range(H_RET):
            dmask[h] = jnp.where(causal, jnp.exp(diff * _log_gamma(h)), 0.0) * scale

    @pl.when(c == 0)
    def _():
        state[...] = jnp.zeros_like(state)

    idx = lax.broadcasted_iota(jnp.int32, (chunk, 1), 0).astype(F32)
    for h in range(H_RET):
        lg = _log_gamma(h)
        q = q_ref[:, h * dk:(h + 1) * dk]
        k = k_ref[:, h * dk:(h + 1) * dk]
        v = v_ref[:, h * dv:(h + 1) * dv]
        s_prev = state[h]
        scores = lax.dot_general(q, k, (((1,), (1,)), ((), ())), preferred_element_type=F32) * dmask[h]
        inner = jnp.dot(scores.astype(BF16), v, preferred_element_type=F32)
        cross = jnp.dot(q, s_prev.astype(BF16), preferred_element_type=F32) * jnp.exp((idx + 1.0) * lg)
        o = inner + cross
        zeta = jnp.exp((chunk - 1.0 - idx) * lg) * scale
        kz_t = (k.astype(F32) * zeta).T.astype(BF16)
        state[h] = math.exp(chunk * lg) * s_prev + jnp.dot(kz_t, v, preferred_element_type=F32)
        oc = o - jnp.mean(o, axis=-1, keepdims=True)
        on = oc * lax.rsqrt(jnp.mean(oc * oc, axis=-1, keepdims=True) + EPS)
        ro_ref[:, h * dv:(h + 1) * dv] = (on * gate_ref[:, h * dv:(h + 1) * dv].astype(F32)).astype(BF16)

    @pl.when(c == pl.num_programs(1) - 1)
    def _():
        sfin_ref[0] = state[...]


def _retention_prompt(rq, rk, rv, gate, batch, seq, dk, dv):
    chunk = RET_CHUNK if seq % RET_CHUNK == 0 else seq
    nc = seq // chunk
    blk = lambda width: pl.BlockSpec((chunk, width), lambda b, c: (b * nc + c, 0))
    return pl.pallas_call(
        functools.partial(_retention_body, dk=dk, dv=dv),
        grid=(batch, nc),
        in_specs=[blk(H_RET * dk), blk(H_RET * dk), blk(H_RET * dv), blk(H_RET * dv)],
        out_specs=[blk(H_RET * dv), pl.BlockSpec((1, H_RET, dk, dv), lambda b, c: (b, 0, 0, 0))],
        out_shape=[jax.ShapeDtypeStruct((batch * seq, H_RET * dv), BF16),
                   jax.ShapeDtypeStruct((batch, H_RET, dk, dv), F32)],
        scratch_shapes=[pltpu.VMEM((H_RET, dk, dv), F32), pltpu.VMEM((H_RET, chunk, chunk), F32)],
        compiler_params=_params(("arbitrary", "arbitrary")),
        name="retention_prompt",
    )(rq, rk, rv, gate)


def _retention_step_body(s_ref, q_ref, k_ref, v_ref, gate_ref, snew_ref, ro_ref, *, dk):
    scale = dk ** -0.5
    for h in range(H_RET):
        s_new = math.exp(_log_gamma(h)) * s_ref[:, h] + (k_ref[:, h] * scale) * v_ref[:, h]
        snew_ref[:, h] = s_new
        o = jnp.sum(q_ref[:, h] * s_new, axis=1, keepdims=True)
        oc = o - jnp.mean(o, axis=-1, keepdims=True)
        on = oc * lax.rsqrt(jnp.mean(oc * oc, axis=-1, keepdims=True) + EPS)
        ro_ref[:, h] = (on * gate_ref[:, h].astype(F32)).astype(BF16)


def _retention_step(state, q_col, k_col, v_row, gate_row, bt):
    nb, _, dk, dv = state.shape
    col = pl.BlockSpec((bt, H_RET, dk, 1), lambda i: (i, 0, 0, 0))
    rowb = pl.BlockSpec((bt, H_RET, 1, dv), lambda i: (i, 0, 0, 0))
    st = pl.BlockSpec((bt, H_RET, dk, dv), lambda i: (i, 0, 0, 0))
    return pl.pallas_call(
        functools.partial(_retention_step_body, dk=dk),
        grid=(nb // bt,),
        in_specs=[st, col, col, rowb, rowb],
        out_specs=[st, rowb],
        out_shape=[jax.ShapeDtypeStruct(state.shape, F32), jax.ShapeDtypeStruct((nb, H_RET, 1, dv), BF16)],
        compiler_params=_params(("parallel",)),
        name="retention_step",
    )(state, q_col, k_col, v_row, gate_row)


def _lambda(lq1_ref, lk1_ref, lq2_ref, lk2_ref, lam_init):
    a = jnp.sum(lq1_ref[...] * lk1_ref[...], axis=-1, keepdims=True)
    b = jnp.sum(lq2_ref[...] * lk2_ref[...], axis=-1, keepdims=True)
    return jnp.exp(a) - jnp.exp(b) + lam_init


def _sub_norm(o, subg, lam_init):
    return o * lax.rsqrt(jnp.mean(o * o, axis=-1, keepdims=True) + EPS) * subg * (1.0 - lam_init)


def _attn_prompt_body(lq1_ref, lk1_ref, lq2_ref, lk2_ref, subg_ref, q_ref, k_ref, v_ref, o_ref, acc_ref,
                      *, dh, lam_init):
    i = pl.program_id(2)
    t = q_ref.shape[0]
    dv = v_ref.shape[1]
    q = q_ref[...]
    lane = lax.broadcasted_iota(jnp.int32, q.shape, 1)
    zero = jnp.zeros_like(q)
    q2 = jnp.concatenate([jnp.where(lane < dh, q, zero), jnp.where(lane >= dh, q, zero)], axis=0)
    row = lax.broadcasted_iota(jnp.int32, (2 * t, t), 0)
    col = lax.broadcasted_iota(jnp.int32, (2 * t, t), 1)
    on_or_below = col <= jnp.where(row >= t, row - t, row)
    ones = jnp.ones((t, LANES), BF16)

    def scores(j):
        kb = k_ref[pl.ds(pl.multiple_of(j * t, t), t), :]
        return lax.dot_general(q2, kb, (((1,), (1,)), ((), ())), preferred_element_type=F32)

    def lane_max(s, m):
        for c in range(t // LANES):
            m = jnp.maximum(m, s[:, c * LANES:(c + 1) * LANES])
        return m

    def accumulate(j, s, mrow):
        p = jnp.exp(s - mrow).astype(BF16)
        v1 = jnp.concatenate([v_ref[pl.ds(pl.multiple_of(j * t, t), t), :], ones], axis=1)
        acc_ref[...] += jnp.dot(p, v1, preferred_element_type=F32)

    s_diag = jnp.where(on_or_below, scores(i), NEG_BIG)
    m = lax.fori_loop(0, i, lambda j, m: lane_max(scores(j), m),
                      lane_max(s_diag, jnp.full((2 * t, LANES), NEG_BIG, F32)))
    mrow = jnp.max(m, axis=-1, keepdims=True)
    acc_ref[...] = jnp.zeros_like(acc_ref)
    accumulate(i, s_diag, mrow)

    def body(j, carry):
        accumulate(j, scores(j), mrow)
        return carry

    lax.fori_loop(0, i, body, 0)
    acc = acc_ref[...]
    o_all = acc[:, :dv] / acc[:, dv:]
    lam = _lambda(lq1_ref, lk1_ref, lq2_ref, lk2_ref, lam_init)
    o = o_all[:t] - lam * o_all[t:]
    o_ref[...] = _sub_norm(o, subg_ref[...], lam_init).astype(BF16)


def _attn_prompt(dq, dkb, dvb, lam_vecs, subg, batch, seq, dh, lam_init):
    t = ATTN_TILE if seq % ATTN_TILE == 0 else seq
    nq = seq // t
    width = 2 * dh
    vec = [_resident(v.shape) for v in lam_vecs]
    return pl.pallas_call(
        functools.partial(_attn_prompt_body, dh=dh, lam_init=lam_init),
        grid=(batch, H_DIFF, nq),
        in_specs=vec + [_resident(subg.shape),
                        pl.BlockSpec((t, width), lambda b, h, i: (b * nq + i, h)),
                        pl.BlockSpec((seq, width), lambda b, h, i: (b, h)),
                        pl.BlockSpec((seq, width), lambda b, h, i: (b, h))],
        out_specs=pl.BlockSpec((t, width), lambda b, h, i: (b * nq + i, h)),
        out_shape=jax.ShapeDtypeStruct((batch * seq, H_DIFF * width), BF16),
        scratch_shapes=[pltpu.VMEM((2 * t, 2 * width), F32)],
        compiler_params=_params(("parallel", "parallel", "arbitrary")),
        name="diff_attn_prompt",
    )(*lam_vecs, subg, dq, dkb, dvb)


class _DecodeSide(NamedTuple):
    first_sample: int
    n_samples: int
    lam_init: float


def _decode_softmax(probs, q8, knew, n_cols):
    piece = DEC_SOFTMAX_PIECE if n_cols % DEC_SOFTMAX_PIECE == 0 else n_cols
    lane = lax.broadcasted_iota(jnp.int32, (2 * H_DIFF, piece), 1)
    row = lax.broadcasted_iota(jnp.int32, (2 * H_DIFF, piece), 0)
    own_head = (lane & (H_DIFF - 1)) == (row & (H_DIFF - 1))
    s_new = jnp.sum(q8.astype(F32) * knew, axis=-1, keepdims=True)
    mx = s_new
    for c in range(n_cols // piece):
        s = jnp.where(own_head, probs[:, c * piece:(c + 1) * piece], NEG_BIG)
        mx = jnp.maximum(mx, jnp.max(s, axis=-1, keepdims=True))
    e_new = jnp.exp(s_new - mx)
    total = e_new
    for c in range(n_cols // piece):
        e = jnp.exp(jnp.where(own_head, probs[:, c * piece:(c + 1) * piece], NEG_BIG) - mx)
        probs[:, c * piece:(c + 1) * piece] = e
        total = total + jnp.sum(e, axis=-1, keepdims=True)
    return e_new, 1.0 / total


def _mix_ffn_body(*refs, widths, projected, side):
    if side is not None:
        pt_ref, refs = refs[0], refs[1:]
    n = len(widths)
    x_ref = refs[0]
    a_refs = refs[1:1 + n]
    if projected:
        wo_ref = None
        g_ref, wup_ref, wdn_ref = refs[1 + n:4 + n]
        rest = refs[4 + n:]
    else:
        wo_ref, g_ref, wup_ref, wdn_ref = refs[1 + n:5 + n]
        rest = refs[5 + n:]
    d_ff = wup_ref.shape[1]
    fc = FF_CHUNK if d_ff % FF_CHUNK == 0 else d_ff
    n_ff = d_ff // fc

    def ffn(after_chunk):
        a = a_refs[0][...] if n == 1 else jnp.concatenate([r[...] for r in a_refs], axis=-1)
        y = x_ref[...] + (a if projected else jnp.dot(a, wo_ref[...], preferred_element_type=F32))
        hb = _rms(y, g_ref[...]).astype(BF16)
        for c in range(n_ff):
            hid = jnp.dot(hb, wup_ref[:, c * fc:(c + 1) * fc], preferred_element_type=F32)
            act = jnp.square(jnp.maximum(hid, 0.0)).astype(BF16)
            y = y + jnp.dot(act, wdn_ref[c * fc:(c + 1) * fc, :], preferred_element_type=F32)
            after_chunk(c)
        y_ref[...] = y

    if side is None:
        (y_ref,) = rest
        ffn(lambda c: None)
        return

    (lq1_ref, lk1_ref, lq2_ref, lk2_ref, subg_ref, q8_ref, knew_ref, vnew_ref, ck_ref, cv_ref,
     y_ref, do_ref, ring, sems, probs, acc) = rest
    step = pl.program_id(0)
    n_steps = pl.num_programs(0)
    slots, cp, page_rows, _ = ring.shape
    n_pages = pt_ref.shape[1]
    n_chunks = n_pages // cp
    per_ff = n_chunks // n_ff
    look = slots - per_ff
    b = step // 2

    def page_copy(phase, at_step, chunk, i):
        sample = side.first_sample + at_step // 2
        src = ck_ref if phase == 0 else cv_ref
        slot = chunk % slots
        return pltpu.make_async_copy(src.at[pt_ref[sample, chunk * cp + i]], ring.at[slot, i], sems.at[slot])

    def start_chunk(phase, at_step, chunk):
        for i in range(cp):
            page_copy(phase, at_step, chunk, i).start()

    def use_chunk(phase, chunk):
        slot = chunk % slots
        q8 = q8_ref[b]
        for i in range(cp):
            cols = slice((chunk * cp + i) * page_rows, (chunk * cp + i + 1) * page_rows)
            pg = ring[slot, i].astype(BF16)
            if phase == 0:
                probs[:, cols] = lax.dot_general(q8, pg, (((1,), (1,)), ((), ())), preferred_element_type=F32)
            else:
                acc[...] += jnp.dot(probs[:, cols].astype(BF16), pg, preferred_element_type=F32)

    def after_chunk(phase, c):
        chunks = range(c * per_ff, (c + 1) * per_ff)
        for chunk in chunks:
            for i in range(cp):
                page_copy(phase, step, chunk, i).wait()
        for chunk in chunks:
            ahead = chunk + look
            if ahead < n_chunks:
                start_chunk(phase, step, ahead)
            else:
                @pl.when(step + 1 < n_steps)
                def _():
                    start_chunk(1 - phase, step + 1, ahead - n_chunks)
        for chunk in chunks:
            use_chunk(phase, chunk)

    @pl.when(step == 0)
    def _():
        for chunk in range(look):
            start_chunk(0, step, chunk)

    @pl.when(step % 2 == 0)
    def _():
        ffn(functools.partial(after_chunk, 0))

    @pl.when(step % 2 == 1)
    def _():
        e_new, inv = _decode_softmax(probs, q8_ref[b], knew_ref[b], n_pages * page_rows)
        acc[...] = jnp.zeros_like(acc)
        ffn(functools.partial(after_chunk, 1))
        vnew = vnew_ref[b]
        a = (acc[...] + e_new * jnp.concatenate([vnew, vnew], axis=0)) * inv
        lam = _lambda(lq1_ref, lk1_ref, lq2_ref, lk2_ref, side.lam_init)
        o = a[:H_DIFF] - lam * a[H_DIFF:]
        do_ref[b] = _sub_norm(o, subg_ref[...], side.lam_init)


def _mix_ffn(x, parts, w_o, g, w_up, w_dn, tm, side=None, decode_ops=None):
    rows, d = x.shape
    widths = tuple(p.shape[1] for p in parts)
    n_steps = rows // tm
    row = lambda width: pl.BlockSpec((tm, width), lambda i, *_: (i, 0))
    weights = ([] if w_o is None else [w_o]) + [g, w_up, w_dn]
    in_specs = [row(d)] + [row(w) for w in widths] + [_resident(w.shape) for w in weights]
    body = functools.partial(_mix_ffn_body, widths=widths, projected=w_o is None, side=side)
    if side is None:
        return pl.pallas_call(
            body, grid=(n_steps,), in_specs=in_specs, out_specs=row(d),
            out_shape=jax.ShapeDtypeStruct((rows, d), F32),
            compiler_params=_params(("parallel",)), name="mix_ffn",
        )(x, *parts, *weights)

    page_table, cache_k, cache_v, q8, knew8, vnew, lam_vecs, subg = decode_ops
    n_pages = page_table.shape[1]
    _, page_rows, width = cache_k.shape
    assert n_steps == 2 * side.n_samples and n_pages % DEC_CHUNK_PAGES == 0
    assert (n_pages // DEC_CHUNK_PAGES) % (w_up.shape[1] // FF_CHUNK) == 0
    lo, hi = side.first_sample, side.first_sample + side.n_samples
    local = [q8[lo:hi], knew8[lo:hi], vnew[lo:hi]]
    any_spec = pl.BlockSpec(memory_space=pl.ANY)
    grid_spec = pltpu.PrefetchScalarGridSpec(
        num_scalar_prefetch=1,
        grid=(n_steps,),
        in_specs=in_specs + [_resident(v.shape) for v in lam_vecs] + [_resident(subg.shape)]
        + [_resident(a.shape) for a in local] + [any_spec, any_spec],
        out_specs=[row(d), pl.BlockSpec((side.n_samples, H_DIFF, width), lambda *_: (0, 0, 0))],
        scratch_shapes=[pltpu.VMEM((DEC_SLOTS, DEC_CHUNK_PAGES, page_rows, width), F32),
                        pltpu.SemaphoreType.DMA((DEC_SLOTS,)),
                        pltpu.VMEM((2 * H_DIFF, n_pages * page_rows), F32),
                        pltpu.VMEM((2 * H_DIFF, width), F32)],
    )
    return pl.pallas_call(
        body, grid_spec=grid_spec,
        out_shape=[jax.ShapeDtypeStruct((rows, d), F32), jax.ShapeDtypeStruct((side.n_samples, H_DIFF, width), F32)],
        compiler_params=_params(("arbitrary",)), name="mix_ffn_decode",
    )(page_table, x, *parts, *weights, *lam_vecs, subg, *local, cache_k, cache_v)


def _gelu(x):
    return 0.5 * x * (1.0 + lax.erf(x * (2.0 ** -0.5)))


def _gmlp_body(x_ref, g_ref, w_ref, lng_ref, lnb_ref, ws_ref, bs_ref, wout_ref, mix_ref, *rest, e, single_token):
    if single_token:
        vn_ref, vbuf, a_ref = rest
    else:
        vbuf, a_ref = rest
    hb = _rms(x_ref[...], g_ref[...]).astype(BF16)
    rows = hb.shape[0]
    gc = e // GMLP_GROUPS
    pc = 2 * gc
    for pair in range(GMLP_GROUPS // 2):
        vbuf[:, pair * pc:(pair + 1) * pc] = _gelu(
            jnp.dot(hb, w_ref[:, e + pair * pc:e + (pair + 1) * pc], preferred_element_type=F32))
    v = vbuf[...]
    vc = v - jnp.mean(v, axis=-1, keepdims=True)
    vn = vc * lax.rsqrt(jnp.mean(vc * vc, axis=-1, keepdims=True) + EPS) * lng_ref[...] + lnb_ref[...]
    vbuf[...] = vn
    if single_token:
        vn_ref[...] = vn
    else:
        t = lax.broadcasted_iota(jnp.int32, (GMLP_CHUNK, GMLP_CHUNK), 0)
        s_ = lax.broadcasted_iota(jnp.int32, (GMLP_CHUNK, GMLP_CHUNK), 1)
        tril = t >= s_
    for grp in range(GMLP_GROUPS):
        cols = slice(grp * gc, (grp + 1) * gc)
        if grp % 2 == 0:
            u2 = _gelu(jnp.dot(hb, w_ref[:, grp * gc:grp * gc + pc], preferred_element_type=F32))
        u = u2[:, (grp % 2) * gc:(grp % 2 + 1) * gc]
        if single_token:
            a_ref[:, cols] = (u * (vbuf[:, cols] * ws_ref[grp][0:1, 0:1] + bs_ref[0:1, grp:grp + 1])).astype(BF16)
        else:
            wsg = jnp.where(tril, ws_ref[grp], 0.0).astype(BF16)
            bias = bs_ref[:, grp:grp + 1]
            for r in range(rows // GMLP_CHUNK):
                rsl = slice(r * GMLP_CHUNK, (r + 1) * GMLP_CHUNK)
                s = jnp.dot(wsg, vbuf[rsl, cols].astype(BF16), preferred_element_type=F32) + bias
                a_ref[rsl, cols] = (u[rsl] * s).astype(BF16)
    mix_ref[...] = jnp.dot(a_ref[...], wout_ref[...], preferred_element_type=F32)


def _gmlp(x, g, w, lng, lnb, ws, bs_t, w_out, tm, single_token):
    rows, d = x.shape
    e = w.shape[1] // 2
    row = lambda width: pl.BlockSpec((tm, width), lambda i: (i, 0))
    out_specs = [row(d)]
    out_shape = [jax.ShapeDtypeStruct((rows, d), F32)]
    if single_token:
        out_specs.append(row(e))
        out_shape.append(jax.ShapeDtypeStruct((rows, e), F32))
    return pl.pallas_call(
        functools.partial(_gmlp_body, e=e, single_token=single_token),
        grid=(rows // tm,),
        in_specs=[row(d), _resident(g.shape), _resident(w.shape), _resident(lng.shape), _resident(lnb.shape),
                  _resident(ws.shape), _resident(bs_t.shape), _resident(w_out.shape)],
        out_specs=out_specs,
        out_shape=out_shape,
        scratch_shapes=[pltpu.VMEM((tm, e), F32), pltpu.VMEM((tm, e), BF16)],
        compiler_params=_params(("parallel",)),
        name="gmlp_sample" if single_token else "gmlp_prompt",
    )(x, g, w, lng, lnb, ws, bs_t, w_out)


def _rope_tables(pos, d):
    inv = 1.0 / (ROPE_THETA ** (jnp.arange(0, d, 2, dtype=F32) / d))
    ang = pos.astype(F32)[:, None] * inv[None, :]
    c, s = jnp.cos(ang), jnp.sin(ang)
    reps = LANES // d
    return jnp.tile(jnp.concatenate([c, c], -1), (1, reps)), jnp.tile(jnp.concatenate([-s, s], -1), (1, reps))


def kernel(x_prompt, x_sample, state_ret, cache_k, cache_v, page_table, rms_mix_g, rms_ffn_g, w_ffn_up, w_ffn_down,
           w_in_even, w_out_even, q_norm_g, k_norm_g, lam_q1, lam_k1, lam_q2, lam_k2, subln_g,
           w_in_odd, ln_v_g, ln_v_b, w_s, b_s, w_out_odd):
    bp, sp, d = x_prompt.shape
    bs, ls, _ = x_sample.shape
    depth = rms_mix_g.shape[0]
    dk_ret, dv_ret = state_ret.shape[-2:]
    n_pool, page_size = cache_k.shape[1:3]
    dh = cache_k.shape[-1] // 2
    dv_diff = cache_v.shape[-1]
    assert dk_ret == LANES and 2 * dh == LANES and dv_diff == LANES and ls == 1
    ret_qk, ret_v = H_RET * dk_ret, H_RET * dv_ret
    diff_qk, diff_v = H_DIFF * 2 * dh, H_DIFF * dv_diff
    dims = (ret_qk, ret_v, diff_qk, diff_v, dh)
    past = page_table.shape[1] * page_size
    tm = ROW_TILE if (bp * sp) % ROW_TILE == 0 and sp % ROW_TILE == 0 else sp
    rows_s = bs * ls

    pos_p = jnp.arange(sp, dtype=jnp.int32)
    pos_s = jnp.tile(past + jnp.arange(ls, dtype=jnp.int32), bs)
    tabs_p = _rope_tables(pos_p, dk_ret) + _rope_tables(pos_p, dh)
    tabs_s = _rope_tables(pos_s, dk_ret) + _rope_tables(pos_s, dh)
    gidx = jnp.arange(2 * LANES) // dh
    grp = (gidx[:, None] == gidx[None, :]).astype(BF16)

    assert depth == 2, "the decode attention of layer 0 is spread over the prompt MLP calls of layers 0 and 1"
    yp = x_prompt.reshape(bp * sp, d)
    ys = x_sample.reshape(rows_s, d)
    ffn_w = [(rms_ffn_g[l][None], w_ffn_up[l].astype(BF16), w_ffn_down[l].astype(BF16)) for l in range(depth)]

    lam_init = 0.8 - 0.6 * math.exp(-0.3 * 0)
    g_mix = rms_mix_g[0][None]
    w_in = w_in_even[0].astype(BF16)
    w_out = w_out_even[0].astype(BF16)
    qg = jnp.tile(q_norm_g[0], LANES // dh)[None]
    kg = jnp.tile(k_norm_g[0], LANES // dh)[None]
    lam_vecs = [lam_q1[0][None], lam_k1[0][None], lam_q2[0][None], lam_k2[0][None]]
    subg = subln_g[0][None]

    rq, rk, rv, gate, dq, dk_p, dv_p, dkb, dvb = _in_proj(yp, g_mix, w_in, tabs_p, sp // tm, qg, kg, grp, tm, dims)
    ro_p, s_fin = _retention_prompt(rq, rk, rv, gate, bp, sp, dk_ret, dv_ret)
    do_p = _attn_prompt(dq, dkb, dvb, lam_vecs, subg, bp, sp, dh, lam_init)

    rq, rk, rv, gate, dq, dk_s, dv_s, _, _ = _in_proj(ys, g_mix, w_in, tabs_s, 1, qg, kg, grp, rows_s, dims)
    s_new, ro_s = _retention_step(
        state_ret[0],
        rq.astype(F32).reshape(bs, H_RET, dk_ret, 1), rk.astype(F32).reshape(bs, H_RET, dk_ret, 1),
        rv.reshape(bs, H_RET, 1, dv_ret), gate.reshape(bs, H_RET, 1, dv_ret), 4 if bs % 4 == 0 else 1)
    q4 = dq.reshape(bs, H_DIFF, 2 * dh)
    half = (jnp.arange(2 * dh) < dh)[None, None, :]
    q8 = jnp.concatenate([jnp.where(half, q4, 0), jnp.where(half, 0, q4)], axis=1)
    k4 = dk_s.reshape(bs, H_DIFF, 2 * dh)
    decode_ops = (page_table,
                  cache_k[0].reshape(n_pool, page_size * H_DIFF, 2 * dh),
                  cache_v[0].reshape(n_pool, page_size * H_DIFF, dv_diff),
                  q8, jnp.concatenate([k4, k4], axis=1), dv_s.reshape(bs, H_DIFF, dv_diff), lam_vecs, subg)
    n_first = bs // 2

    yp, do_a = _mix_ffn(yp, [ro_p, do_p], w_out, *ffn_w[0], tm, _DecodeSide(0, n_first, lam_init), decode_ops)

    g_mix1 = rms_mix_g[1][None]
    w_in1 = w_in_odd[0].astype(BF16)
    w_out1 = w_out_odd[0].astype(BF16)
    lng, lnb = ln_v_g[0][None], ln_v_b[0][None]
    bs_t = b_s[0].T
    (mix,) = _gmlp(yp, g_mix1, w_in1, lng, lnb, w_s[0], bs_t, w_out1, min(tm, GMLP_TILE), False)
    yp, do_b = _mix_ffn(yp, [mix], None, *ffn_w[1], tm, _DecodeSide(n_first, bs - n_first, lam_init), decode_ops)

    do_s = jnp.concatenate([do_a, do_b], axis=0).astype(BF16)
    ys = _mix_ffn(ys, [ro_s.reshape(rows_s, ret_v), do_s.reshape(rows_s, diff_v)], w_out, *ffn_w[0], rows_s)
    mix, gv = _gmlp(ys, g_mix1, w_in1, lng, lnb, w_s[0], bs_t, w_out1, rows_s, True)
    ys = _mix_ffn(ys, [mix], None, *ffn_w[1], rows_s)

    return (yp.reshape(bp, sp, d), ys.reshape(bs, ls, d), s_fin[None], s_new[None],
            dk_p.reshape(1, bp, sp, H_DIFF, 2 * dh), dv_p.reshape(1, bp, sp, H_DIFF, dv_diff),
            dk_s.reshape(1, bs, ls, H_DIFF, 2 * dh), dv_s.reshape(1, bs, ls, H_DIFF, dv_diff),
            gv.reshape(1, bs, ls, -1))
```

```python
import functools
import math
from typing import NamedTuple

import jax
import jax.numpy as jnp
from jax import lax
from jax.experimental import pallas as pl
from jax.experimental.pallas import tpu as pltpu

F32 = jnp.float32
BF16 = jnp.bfloat16
EPS = 1e-6
ROPE_THETA = 10000.0
H_RET = 4
H_DIFF = 4
GMLP_GROUPS = 8
GMLP_CHUNK = 128
LANES = 128
VMEM_LIMIT_BYTES = 58 * 1024 * 1024
NEG_BIG = -1e30

ROW_TILE = 512
GMLP_TILE = 256
RET_CHUNK = 256
ATTN_TILE = 512
FF_CHUNK = 512
DEC_CHUNK_PAGES = 8
DEC_SLOTS = 4
DEC_SOFTMAX_PIECE = 8192


def _params(semantics):
    return pltpu.CompilerParams(dimension_semantics=semantics, vmem_limit_bytes=VMEM_LIMIT_BYTES)


def _resident(shape):
    return pl.BlockSpec(shape, lambda *_: (0,) * len(shape), pipeline_mode=pl.Buffered(1))


def _rms(x, g):
    return x * lax.rsqrt(jnp.mean(x * x, axis=-1, keepdims=True) + EPS) * g


def _log_gamma(h):
    return math.log1p(-(2.0 ** (-5.0 - h)))


def _in_proj_body(x_ref, g_ref, w_ref, cr_ref, sr_ref, cd_ref, sd_ref, qg_ref, kg_ref, grp_ref,
                  rq_ref, rk_ref, rv_ref, gate_ref, dq_ref, dk_ref, dv_ref, dkb_ref, dvb_ref,
                  *, ret_qk, ret_v, diff_qk, diff_v, dh):
    hb = _rms(x_ref[...], g_ref[...]).astype(BF16)
    rows = hb.shape[0]

    def proj(lo, width):
        return jnp.dot(hb, w_ref[:, lo:lo + width], preferred_element_type=F32)

    cr, sr = cr_ref[...], sr_ref[...]
    for col0, out_ref in ((0, rq_ref), (ret_qk, rk_ref)):
        p = proj(col0, ret_qk)
        for c in range(ret_qk // LANES):
            s = p[:, c * LANES:(c + 1) * LANES]
            out_ref[:, c * LANES:(c + 1) * LANES] = (s * cr + pltpu.roll(s, LANES // 2, 1) * sr).astype(BF16)
    rv_ref[...] = proj(2 * ret_qk, ret_v).astype(BF16)
    rg = proj(2 * ret_qk + ret_v, ret_v)
    gate_ref[...] = (rg * jax.nn.sigmoid(rg)).astype(BF16)

    cd, sd = cd_ref[...], sd_ref[...]
    lane = lax.broadcasted_iota(jnp.int32, (rows, LANES), 1)
    first_half = (lane & (dh // 2)) == 0
    grp = grp_ref[...]
    gw = grp.shape[0]
    base = 2 * ret_qk + 2 * ret_v

    def norm_rope(col0, gain):
        p = proj(col0, diff_qk)
        slabs = []
        for c in range(diff_qk // gw):
            s = p[:, c * gw:(c + 1) * gw]
            ssq = jnp.dot((s * s).astype(BF16), grp, preferred_element_type=F32)
            n = s * lax.rsqrt(ssq * (1.0 / dh) + EPS)
            for d in range(gw // LANES):
                t = n[:, d * LANES:(d + 1) * LANES] * gain
                partner = jnp.where(first_half, pltpu.roll(t, LANES - dh // 2, 1), pltpu.roll(t, dh // 2, 1))
                slabs.append(t * cd + partner * sd)
        return slabs

    q_scale = dh ** -0.5
    for c, s in enumerate(norm_rope(base, qg_ref[...])):
        dq_ref[:, c * LANES:(c + 1) * LANES] = (s * q_scale).astype(BF16)
    for c, s in enumerate(norm_rope(base + diff_qk, kg_ref[...])):
        dk_ref[:, c, :] = s
        dkb_ref[:, c * LANES:(c + 1) * LANES] = s.astype(BF16)
    dv = proj(base + 2 * diff_qk, diff_v)
    for c in range(diff_v // LANES):
        dv_ref[:, c, :] = dv[:, c * LANES:(c + 1) * LANES]
    dvb_ref[...] = dv.astype(BF16)


def _in_proj(x, g, w, tabs, tab_blocks, qg, kg, grp, tm, dims):
    rows, d = x.shape
    ret_qk, ret_v, diff_qk, diff_v, dh = dims
    n = rows // tm
    row = lambda width: pl.BlockSpec((tm, width), lambda i: (i, 0))
    tab = pl.BlockSpec((tm, LANES), lambda i: (i % tab_blocks, 0))
    body = functools.partial(_in_proj_body, ret_qk=ret_qk, ret_v=ret_v, diff_qk=diff_qk, diff_v=diff_v, dh=dh)
    shapes = [((rows, ret_qk), BF16), ((rows, ret_qk), BF16), ((rows, ret_v), BF16), ((rows, ret_v), BF16),
              ((rows, diff_qk), BF16), ((rows, diff_qk // LANES, LANES), F32), ((rows, diff_v // LANES, LANES), F32),
              ((rows, diff_qk), BF16), ((rows, diff_v), BF16)]
    out_spec = lambda s: row(s[1]) if len(s) == 2 else pl.BlockSpec((tm,) + s[1:], lambda i: (i, 0, 0))
    return pl.pallas_call(
        body,
        grid=(n,),
        in_specs=[row(d), _resident(g.shape), _resident(w.shape), tab, tab, tab, tab,
                  _resident(qg.shape), _resident(kg.shape), _resident(grp.shape)],
        out_specs=[out_spec(s) for s, _ in shapes],
        out_shape=[jax.ShapeDtypeStruct(s, t) for s, t in shapes],
        compiler_params=_params(("parallel",)),
        name="in_proj",
    )(x, g, w, *tabs, qg, kg, grp)


def _retention_body(q_ref, k_ref, v_ref, gate_ref, ro_ref, sfin_ref, state, dmask, *, dk, dv):
    b, c = pl.program_id(0), pl.program_id(1)
    chunk = q_ref.shape[0]
    scale = dk ** -0.5

    @pl.when((b == 0) & (c == 0))
    def _():
        i = lax.broadcasted_iota(jnp.int32, (chunk, chunk), 0)
        j = lax.broadcasted_iota(jnp.int32, (chunk, chunk), 1)
        causal = i >= j
        diff = jnp.where(causal, i - j, 0).astype(F32)
        for h in range(H_RET):
            dmask[h] = jnp.where(causal, jnp.exp(diff * _log_gamma(h)), 0.0) * scale

    @pl.when(c == 0)
    def _():
        state[...] = jnp.zeros_like(state)

    idx = lax.broadcasted_iota(jnp.int32, (chunk, 1), 0).astype(F32)
    for h in range(H_RET):
        lg = _log_gamma(h)
        q = q_ref[:, h * dk:(h + 1) * dk]
        k = k_ref[:, h * dk:(h + 1) * dk]
        v = v_ref[:, h * dv:(h + 1) * dv]
        s_prev = state[h]
        scores = lax.dot_general(q, k, (((1,), (1,)), ((), ())), preferred_element_type=F32) * dmask[h]
        inner = jnp.dot(scores.astype(BF16), v, preferred_element_type=F32)
        cross = jnp.dot(q, s_prev.astype(BF16), preferred_element_type=F32) * jnp.exp((idx + 1.0) * lg)
        o = inner + cross
        zeta = jnp.exp((chunk - 1.0 - idx) * lg) * scale
        kz_t = (k.astype(F32) * zeta).T.astype(BF16)
        state[h] = math.exp(chunk * lg) * s_prev + jnp.dot(kz_t, v, preferred_element_type=F32)
        oc = o - jnp.mean(o, axis=-1, keepdims=True)
        on = oc * lax.rsqrt(jnp.mean(oc * oc, axis=-1, keepdims=True) + EPS)
        ro_ref[:, h * dv:(h + 1) * dv] = (on * gate_ref[:, h * dv:(h + 1) * dv].astype(F32)).astype(BF16)

    @pl.when(c == pl.num_programs(1) - 1)
    def _():
        sfin_ref[0] = state[...]


def _retention_prompt(rq, rk, rv, gate, batch, seq, dk, dv):
    chunk = RET_CHUNK if seq % RET_CHUNK == 0 else seq
    nc = seq // chunk
    blk = lambda width: pl.BlockSpec((chunk, width), lambda b, c: (b * nc + c, 0))
    return pl.pallas_call(
        functools.partial(_retention_body, dk=dk, dv=dv),
        grid=(batch, nc),
        in_specs=[blk(H_RET * dk), blk(H_RET * dk), blk(H_RET * dv), blk(H_RET * dv)],
        out_specs=[blk(H_RET * dv), pl.BlockSpec((1, H_RET, dk, dv), lambda b, c: (b, 0, 0, 0))],
        out_shape=[jax.ShapeDtypeStruct((batch * seq, H_RET * dv), BF16),
                   jax.ShapeDtypeStruct((batch, H_RET, dk, dv), F32)],
        scratch_shapes=[pltpu.VMEM((H_RET, dk, dv), F32), pltpu.VMEM((H_RET, chunk, chunk), F32)],
        compiler_params=_params(("arbitrary", "arbitrary")),
        name="retention_prompt",
    )(rq, rk, rv, gate)


def _retention_step_body(s_ref, q_ref, k_ref, v_ref, gate_ref, snew_ref, ro_ref, *, dk):
    scale = dk ** -0.5
    for h in range(H_RET):
        s_new = math.exp(_log_gamma(h)) * s_ref[:, h] + (k_ref[:, h] * scale) * v_ref[:, h]
        snew_ref[:, h] = s_new
        o = jnp.sum(q_ref[:, h] * s_new, axis=1, keepdims=True)
        oc = o - jnp.mean(o, axis=-1, keepdims=True)
        on = oc * lax.rsqrt(jnp.mean(oc * oc, axis=-1, keepdims=True) + EPS)
        ro_ref[:, h] = (on * gate_ref[:, h].astype(F32)).astype(BF16)


def _retention_step(state, q_col, k_col, v_row, gate_row, bt):
    nb, _, dk, dv = state.shape
    col = pl.BlockSpec((bt, H_RET, dk, 1), lambda i: (i, 0, 0, 0))
    rowb = pl.BlockSpec((bt, H_RET, 1, dv), lambda i: (i, 0, 0, 0))
    st = pl.BlockSpec((bt, H_RET, dk, dv), lambda i: (i, 0, 0, 0))
    return pl.pallas_call(
        functools.partial(_retention_step_body, dk=dk),
        grid=(nb // bt,),
        in_specs=[st, col, col, rowb, rowb],
        out_specs=[st, rowb],
        out_shape=[jax.ShapeDtypeStruct(state.shape, F32), jax.ShapeDtypeStruct((nb, H_RET, 1, dv), BF16)],
        compiler_params=_params(("parallel",)),
        name="retention_step",
    )(state, q_col, k_col, v_row, gate_row)


def _lambda(lq1_ref, lk1_ref, lq2_ref, lk2_ref, lam_init):
    a = jnp.sum(lq1_ref[...] * lk1_ref[...], axis=-1, keepdims=True)
    b = jnp.sum(lq2_ref[...] * lk2_ref[...], axis=-1, keepdims=True)
    return jnp.exp(a) - jnp.exp(b) + lam_init


def _sub_norm(o, subg, lam_init):
    return o * lax.rsqrt(jnp.mean(o * o, axis=-1, keepdims=True) + EPS) * subg * (1.0 - lam_init)


def _attn_prompt_body(lq1_ref, lk1_ref, lq2_ref, lk2_ref, subg_ref, q_ref, k_ref, v_ref, o_ref, acc_ref,
                      *, dh, lam_init):
    i = pl.program_id(2)
    t = q_ref.shape[0]
    dv = v_ref.shape[1]
    q = q_ref[...]
    lane = lax.broadcasted_iota(jnp.int32, q.shape, 1)
    zero = jnp.zeros_like(q)
    q2 = jnp.concatenate([jnp.where(lane < dh, q, zero), jnp.where(lane >= dh, q, zero)], axis=0)
    row = lax.broadcasted_iota(jnp.int32, (2 * t, t), 0)
    col = lax.broadcasted_iota(jnp.int32, (2 * t, t), 1)
    on_or_below = col <= jnp.where(row >= t, row - t, row)
    ones = jnp.ones((t, LANES), BF16)

    def scores(j):
        kb = k_ref[pl.ds(pl.multiple_of(j * t, t), t), :]
        return lax.dot_general(q2, kb, (((1,), (1,)), ((), ())), preferred_element_type=F32)

    def lane_max(s, m):
        for c in range(t // LANES):
            m = jnp.maximum(m, s[:, c * LANES:(c + 1) * LANES])
        return m

    def accumulate(j, s, mrow):
        p = jnp.exp(s - mrow).astype(BF16)
        v1 = jnp.concatenate([v_ref[pl.ds(pl.multiple_of(j * t, t), t), :], ones], axis=1)
        acc_ref[...] += jnp.dot(p, v1, preferred_element_type=F32)

    s_diag = jnp.where(on_or_below, scores(i), NEG_BIG)
    m = lax.fori_loop(0, i, lambda j, m: lane_max(scores(j), m),
                      lane_max(s_diag, jnp.full((2 * t, LANES), NEG_BIG, F32)))
    mrow = jnp.max(m, axis=-1, keepdims=True)
    acc_ref[...] = jnp.zeros_like(acc_ref)
    accumulate(i, s_diag, mrow)

    def body(j, carry):
        accumulate(j, scores(j), mrow)
        return carry

    lax.fori_loop(0, i, body, 0)
    acc = acc_ref[...]
    o_all = acc[:, :dv] / acc[:, dv:]
    lam = _lambda(lq1_ref, lk1_ref, lq2_ref, lk2_ref, lam_init)
    o = o_all[:t] - lam * o_all[t:]
    o_ref[...] = _sub_norm(o, subg_ref[...], lam_init).astype(BF16)


def _attn_prompt(dq, dkb, dvb, lam_vecs, subg, batch, seq, dh, lam_init):
    t = ATTN_TILE if seq % ATTN_TILE == 0 else seq
    nq = seq // t
    width = 2 * dh
    vec = [_resident(v.shape) for v in lam_vecs]
    return pl.pallas_call(
        functools.partial(_attn_prompt_body, dh=dh, lam_init=lam_init),
        grid=(batch, H_DIFF, nq),
        in_specs=vec + [_resident(subg.shape),
                        pl.BlockSpec((t, width), lambda b, h, i: (b * nq + i, h)),
                        pl.BlockSpec((seq, width), lambda b, h, i: (b, h)),
                        pl.BlockSpec((seq, width), lambda b, h, i: (b, h))],
        out_specs=pl.BlockSpec((t, width), lambda b, h, i: (b * nq + i, h)),
        out_shape=jax.ShapeDtypeStruct((batch * seq, H_DIFF * width), BF16),
        scratch_shapes=[pltpu.VMEM((2 * t, 2 * width), F32)],
        compiler_params=_params(("parallel", "parallel", "arbitrary")),
        name="diff_attn_prompt",
    )(*lam_vecs, subg, dq, dkb, dvb)


class _DecodeSide(NamedTuple):
    first_sample: int
    n_samples: int
    lam_init: float


def _decode_softmax(probs, q8, knew, n_cols):
    piece = DEC_SOFTMAX_PIECE if n_cols % DEC_SOFTMAX_PIECE == 0 else n_cols
    lane = lax.broadcasted_iota(jnp.int32, (2 * H_DIFF, piece), 1)
    row = lax.broadcasted_iota(jnp.int32, (2 * H_DIFF, piece), 0)
    own_head = (lane & (H_DIFF - 1)) == (row & (H_DIFF - 1))
    s_new = jnp.sum(q8.astype(F32) * knew, axis=-1, keepdims=True)
    mx = s_new
    for c in range(n_cols // piece):
        s = jnp.where(own_head, probs[:, c * piece:(c + 1) * piece], NEG_BIG)
        mx = jnp.maximum(mx, jnp.max(s, axis=-1, keepdims=True))
    e_new = jnp.exp(s_new - mx)
    total = e_new
    for c in range(n_cols // piece):
        e = jnp.exp(jnp.where(own_head, probs[:, c * piece:(c + 1) * piece], NEG_BIG) - mx)
        probs[:, c * piece:(c + 1) * piece] = e
        total = total + jnp.sum(e, axis=-1, keepdims=True)
    return e_new, 1.0 / total


def _mix_ffn_body(*refs, widths, projected, side):
    if side is not None:
        pt_ref, refs = refs[0], refs[1:]
    n = len(widths)
    x_ref = refs[0]
    a_refs = refs[1:1 + n]
    if projected:
        wo_ref = None
        g_ref, wup_ref, wdn_ref = refs[1 + n:4 + n]
        rest = refs[4 + n:]
    else:
        wo_ref, g_ref, wup_ref, wdn_ref = refs[1 + n:5 + n]
        rest = refs[5 + n:]
    d_ff = wup_ref.shape[1]
    fc = FF_CHUNK if d_ff % FF_CHUNK == 0 else d_ff
    n_ff = d_ff // fc

    def ffn(after_chunk):
        a = a_refs[0][...] if n == 1 else jnp.concatenate([r[...] for r in a_refs], axis=-1)
        y = x_ref[...] + (a if projected else jnp.dot(a, wo_ref[...], preferred_element_type=F32))
        hb = _rms(y, g_ref[...]).astype(BF16)
        for c in range(n_ff):
            hid = jnp.dot(hb, wup_ref[:, c * fc:(c + 1) * fc], preferred_element_type=F32)
            act = jnp.square(jnp.maximum(hid, 0.0)).astype(BF16)
            y = y + jnp.dot(act, wdn_ref[c * fc:(c + 1) * fc, :], preferred_element_type=F32)
            after_chunk(c)
        y_ref[...] = y

    if side is None:
        (y_ref,) = rest
        ffn(lambda c: None)
        return

    (lq1_ref, lk1_ref, lq2_ref, lk2_ref, subg_ref, q8_ref, knew_ref, vnew_ref, ck_ref, cv_ref,
     y_ref, do_ref, ring, sems, probs, acc) = rest
    step = pl.program_id(0)
    n_steps = pl.num_programs(0)
    slots, cp, page_rows, _ = ring.shape
    n_pages = pt_ref.shape[1]
    n_chunks = n_pages // cp
    per_ff = n_chunks // n_ff
    look = slots
    b = step // 2

    def page_copy(phase, at_step, chunk, i):
        sample = side.first_sample + at_step // 2
        src = ck_ref if phase == 0 else cv_ref
        slot = chunk % slots
        return pltpu.make_async_copy(src.at[pt_ref[sample, chunk * cp + i]], ring.at[slot, i], sems.at[slot])

    def start_chunk(phase, at_step, chunk):
        for i in range(cp):
            page_copy(phase, at_step, chunk, i).start()

    def use_chunk(phase, chunk):
        cols = slice(chunk * cp * page_rows, (chunk + 1) * cp * page_rows)
        pages = ring[chunk % slots].reshape(cp * page_rows, ring.shape[-1]).astype(BF16)
        if phase == 0:
            probs[:, cols] = lax.dot_general(q8_ref[b], pages, (((1,), (1,)), ((), ())), preferred_element_type=F32)
        else:
            acc[...] += jnp.dot(probs[:, cols].astype(BF16), pages, preferred_element_type=F32)

    def after_chunk(phase, c):
        chunks = range(c * per_ff, (c + 1) * per_ff)
        for chunk in chunks:
            for i in range(cp):
                page_copy(phase, step, chunk, i).wait()
        for chunk in chunks:
            use_chunk(phase, chunk)
        for chunk in chunks:
            ahead = chunk + look
            if ahead < n_chunks:
                start_chunk(phase, step, ahead)
            else:
                @pl.when(step + 1 < n_steps)
                def _():
                    start_chunk(1 - phase, step + 1, ahead - n_chunks)

    @pl.when(step == 0)
    def _():
        for chunk in range(look):
            start_chunk(0, step, chunk)

    @pl.when(step % 2 == 0)
    def _():
        ffn(functools.partial(after_chunk, 0))

    @pl.when(step % 2 == 1)
    def _():
        e_new, inv = _decode_softmax(probs, q8_ref[b], knew_ref[b], n_pages * page_rows)
        acc[...] = jnp.zeros_like(acc)
        ffn(functools.partial(after_chunk, 1))
        vnew = vnew_ref[b]
        a = (acc[...] + e_new * jnp.concatenate([vnew, vnew], axis=0)) * inv
        lam = _lambda(lq1_ref, lk1_ref, lq2_ref, lk2_ref, side.lam_init)
        o = a[:H_DIFF] - lam * a[H_DIFF:]
        do_ref[b] = _sub_norm(o, subg_ref[...], side.lam_init)


def _mix_ffn(x, parts, w_o, g, w_up, w_dn, tm, side=None, decode_ops=None):
    rows, d = x.shape
    widths = tuple(p.shape[1] for p in parts)
    n_steps = rows // tm
    row = lambda width: pl.BlockSpec((tm, width), lambda i, *_: (i, 0))
    weights = ([] if w_o is None else [w_o]) + [g, w_up, w_dn]
    in_specs = [row(d)] + [row(w) for w in widths] + [_resident(w.shape) for w in weights]
    body = functools.partial(_mix_ffn_body, widths=widths, projected=w_o is None, side=side)
    if side is None:
        return pl.pallas_call(
            body, grid=(n_steps,), in_specs=in_specs, out_specs=row(d),
            out_shape=jax.ShapeDtypeStruct((rows, d), F32),
            compiler_params=_params(("parallel",)), name="mix_ffn",
        )(x, *parts, *weights)

    page_table, cache_k, cache_v, q8, knew8, vnew, lam_vecs, subg = decode_ops
    n_pages = page_table.shape[1]
    _, page_rows, width = cache_k.shape
    assert n_steps == 2 * side.n_samples and n_pages % DEC_CHUNK_PAGES == 0
    assert (n_pages // DEC_CHUNK_PAGES) % (w_up.shape[1] // FF_CHUNK) == 0
    assert (n_pages // DEC_CHUNK_PAGES) % DEC_SLOTS == 0
    lo, hi = side.first_sample, side.first_sample + side.n_samples
    local = [q8[lo:hi], knew8[lo:hi], vnew[lo:hi]]
    any_spec = pl.BlockSpec(memory_space=pl.ANY)
    grid_spec = pltpu.PrefetchScalarGridSpec(
        num_scalar_prefetch=1,
        grid=(n_steps,),
        in_specs=in_specs + [_resident(v.shape) for v in lam_vecs] + [_resident(subg.shape)]
        + [_resident(a.shape) for a in local] + [any_spec, any_spec],
        out_specs=[row(d), pl.BlockSpec((side.n_samples, H_DIFF, width), lambda *_: (0, 0, 0))],
        scratch_shapes=[pltpu.VMEM((DEC_SLOTS, DEC_CHUNK_PAGES, page_rows, width), F32),
                        pltpu.SemaphoreType.DMA((DEC_SLOTS,)),
                        pltpu.VMEM((2 * H_DIFF, n_pages * page_rows), F32),
                        pltpu.VMEM((2 * H_DIFF, width), F32)],
    )
    return pl.pallas_call(
        body, grid_spec=grid_spec,
        out_shape=[jax.ShapeDtypeStruct((rows, d), F32), jax.ShapeDtypeStruct((side.n_samples, H_DIFF, width), F32)],
        compiler_params=_params(("arbitrary",)), name="mix_ffn_decode",
    )(page_table, x, *parts, *weights, *lam_vecs, subg, *local, cache_k, cache_v)


def _gelu(x):
    return 0.5 * x * (1.0 + lax.erf(x * (2.0 ** -0.5)))


def _gmlp_body(x_ref, g_ref, w_ref, lng_ref, lnb_ref, ws_ref, bs_ref, wout_ref, mix_ref, *rest, e, single_token):
    if single_token:
        vn_ref, vbuf, a_ref = rest
    else:
        vbuf, a_ref = rest
    hb = _rms(x_ref[...], g_ref[...]).astype(BF16)
    rows = hb.shape[0]
    gc = e // GMLP_GROUPS
    pc = 2 * gc
    for pair in range(GMLP_GROUPS // 2):
        vbuf[:, pair * pc:(pair + 1) * pc] = _gelu(
            jnp.dot(hb, w_ref[:, e + pair * pc:e + (pair + 1) * pc], preferred_element_type=F32))
    v = vbuf[...]
    vc = v - jnp.mean(v, axis=-1, keepdims=True)
    vn = vc * lax.rsqrt(jnp.mean(vc * vc, axis=-1, keepdims=True) + EPS) * lng_ref[...] + lnb_ref[...]
    vbuf[...] = vn
    if single_token:
        vn_ref[...] = vn
    else:
        t = lax.broadcasted_iota(jnp.int32, (GMLP_CHUNK, GMLP_CHUNK), 0)
        s_ = lax.broadcasted_iota(jnp.int32, (GMLP_CHUNK, GMLP_CHUNK), 1)
        tril = t >= s_
    for grp in range(GMLP_GROUPS):
        cols = slice(grp * gc, (grp + 1) * gc)
        if grp % 2 == 0:
            u2 = _gelu(jnp.dot(hb, w_ref[:, grp * gc:grp * gc + pc], preferred_element_type=F32))
        u = u2[:, (grp % 2) * gc:(grp % 2 + 1) * gc]
        if single_token:
            a_ref[:, cols] = (u * (vbuf[:, cols] * ws_ref[grp][0:1, 0:1] + bs_ref[0:1, grp:grp + 1])).astype(BF16)
        else:
            wsg = jnp.where(tril, ws_ref[grp], 0.0).astype(BF16)
            bias = bs_ref[:, grp:grp + 1]
            for r in range(rows // GMLP_CHUNK):
                rsl = slice(r * GMLP_CHUNK, (r + 1) * GMLP_CHUNK)
                s = jnp.dot(wsg, vbuf[rsl, cols].astype(BF16), preferred_element_type=F32) + bias
                a_ref[rsl, cols] = (u[rsl] * s).astype(BF16)
    mix_ref[...] = jnp.dot(a_ref[...], wout_ref[...], preferred_element_type=F32)


def _gmlp(x, g, w, lng, lnb, ws, bs_t, w_out, tm, single_token):
    rows, d = x.shape
    e = w.shape[1] // 2
    row = lambda width: pl.BlockSpec((tm, width), lambda i: (i, 0))
    out_specs = [row(d)]
    out_shape = [jax.ShapeDtypeStruct((rows, d), F32)]
    if single_token:
        out_specs.append(row(e))
        out_shape.append(jax.ShapeDtypeStruct((rows, e), F32))
    return pl.pallas_call(
        functools.partial(_gmlp_body, e=e, single_token=single_token),
        grid=(rows // tm,),
        in_specs=[row(d), _resident(g.shape), _resident(w.shape), _resident(lng.shape), _resident(lnb.shape),
                  _resident(ws.shape), _resident(bs_t.shape), _resident(w_out.shape)],
        out_specs=out_specs,
        out_shape=out_shape,
        scratch_shapes=[pltpu.VMEM((tm, e), F32), pltpu.VMEM((tm, e), BF16)],
        compiler_params=_params(("parallel",)),
        name="gmlp_sample" if single_token else "gmlp_prompt",
    )(x, g, w, lng, lnb, ws, bs_t, w_out)


def _rope_tables(pos, d):
    inv = 1.0 / (ROPE_THETA ** (jnp.arange(0, d, 2, dtype=F32) / d))
    ang = pos.astype(F32)[:, None] * inv[None, :]
    c, s = jnp.cos(ang), jnp.sin(ang)
    reps = LANES // d
    return jnp.tile(jnp.concatenate([c, c], -1), (1, reps)), jnp.tile(jnp.concatenate([-s, s], -1), (1, reps))


def kernel(x_prompt, x_sample, state_ret, cache_k, cache_v, page_table, rms_mix_g, rms_ffn_g, w_ffn_up, w_ffn_down,
           w_in_even, w_out_even, q_norm_g, k_norm_g, lam_q1, lam_k1, lam_q2, lam_k2, subln_g,
           w_in_odd, ln_v_g, ln_v_b, w_s, b_s, w_out_odd):
    bp, sp, d = x_prompt.shape
    bs, ls, _ = x_sample.shape
    depth = rms_mix_g.shape[0]
    dk_ret, dv_ret = state_ret.shape[-2:]
    n_pool, page_size = cache_k.shape[1:3]
    dh = cache_k.shape[-1] // 2
    dv_diff = cache_v.shape[-1]
    assert dk_ret == LANES and 2 * dh == LANES and dv_diff == LANES and ls == 1
    ret_qk, ret_v = H_RET * dk_ret, H_RET * dv_ret
    diff_qk, diff_v = H_DIFF * 2 * dh, H_DIFF * dv_diff
    dims = (ret_qk, ret_v, diff_qk, diff_v, dh)
    past = page_table.shape[1] * page_size
    tm = ROW_TILE if (bp * sp) % ROW_TILE == 0 and sp % ROW_TILE == 0 else sp
    rows_s = bs * ls

    pos_p = jnp.arange(sp, dtype=jnp.int32)
    pos_s = jnp.tile(past + jnp.arange(ls, dtype=jnp.int32), bs)
    tabs_p = _rope_tables(pos_p, dk_ret) + _rope_tables(pos_p, dh)
    tabs_s = _rope_tables(pos_s, dk_ret) + _rope_tables(pos_s, dh)
    gidx = jnp.arange(2 * LANES) // dh
    grp = (gidx[:, None] == gidx[None, :]).astype(BF16)

    assert depth == 2, "the decode attention of layer 0 is spread over the prompt MLP calls of layers 0 and 1"
    yp = x_prompt.reshape(bp * sp, d)
    ys = x_sample.reshape(rows_s, d)
    ffn_w = [(rms_ffn_g[l][None], w_ffn_up[l].astype(BF16), w_ffn_down[l].astype(BF16)) for l in range(depth)]

    lam_init = 0.8 - 0.6 * math.exp(-0.3 * 0)
    g_mix = rms_mix_g[0][None]
    w_in = w_in_even[0].astype(BF16)
    w_out = w_out_even[0].astype(BF16)
    qg = jnp.tile(q_norm_g[0], LANES // dh)[None]
    kg = jnp.tile(k_norm_g[0], LANES // dh)[None]
    lam_vecs = [lam_q1[0][None], lam_k1[0][None], lam_q2[0][None], lam_k2[0][None]]
    subg = subln_g[0][None]

    rq, rk, rv, gate, dq, dk_p, dv_p, dkb, dvb = _in_proj(yp, g_mix, w_in, tabs_p, sp // tm, qg, kg, grp, tm, dims)
    ro_p, s_fin = _retention_prompt(rq, rk, rv, gate, bp, sp, dk_ret, dv_ret)
    do_p = _attn_prompt(dq, dkb, dvb, lam_vecs, subg, bp, sp, dh, lam_init)

    rq, rk, rv, gate, dq, dk_s, dv_s, _, _ = _in_proj(ys, g_mix, w_in, tabs_s, 1, qg, kg, grp, rows_s, dims)
    s_new, ro_s = _retention_step(
        state_ret[0],
        rq.astype(F32).reshape(bs, H_RET, dk_ret, 1), rk.astype(F32).reshape(bs, H_RET, dk_ret, 1),
        rv.reshape(bs, H_RET, 1, dv_ret), gate.reshape(bs, H_RET, 1, dv_ret), 4 if bs % 4 == 0 else 1)
    q4 = dq.reshape(bs, H_DIFF, 2 * dh)
    half = (jnp.arange(2 * dh) < dh)[None, None, :]
    q8 = jnp.concatenate([jnp.where(half, q4, 0), jnp.where(half, 0, q4)], axis=1)
    k4 = dk_s
    decode_ops = (page_table,
                  cache_k[0].reshape(n_pool, page_size * H_DIFF, 2 * dh),
                  cache_v[0].reshape(n_pool, page_size * H_DIFF, dv_diff),
                  q8, jnp.concatenate([k4, k4], axis=1), dv_s.reshape(bs, H_DIFF, dv_diff), lam_vecs, subg)
    n_first = bs // 2

    yp, do_a = _mix_ffn(yp, [ro_p, do_p], w_out, *ffn_w[0], tm, _DecodeSide(0, n_first, lam_init), decode_ops)

    g_mix1 = rms_mix_g[1][None]
    w_in1 = w_in_odd[0].astype(BF16)
    w_out1 = w_out_odd[0].astype(BF16)
    lng, lnb = ln_v_g[0][None], ln_v_b[0][None]
    bs_t = b_s[0].T
    (mix,) = _gmlp(yp, g_mix1, w_in1, lng, lnb, w_s[0], bs_t, w_out1, min(tm, GMLP_TILE), False)
    yp, do_b = _mix_ffn(yp, [mix], None, *ffn_w[1], tm, _DecodeSide(n_first, bs - n_first, lam_init), decode_ops)

    do_s = jnp.concatenate([do_a, do_b], axis=0).astype(BF16)
    ys = _mix_ffn(ys, [ro_s.reshape(rows_s, ret_v), do_s.reshape(rows_s, diff_v)], w_out, *ffn_w[0], rows_s)
    mix, gv = _gmlp(ys, g_mix1, w_in1, lng, lnb, w_s[0], bs_t, w_out1, rows_s, True)
    ys = _mix_ffn(ys, [mix], None, *ffn_w[1], rows_s)

    return (yp.reshape(bp, sp, d), ys.reshape(bs, ls, d), s_fin[None], s_new[None],
            dk_p.reshape(1, bp, sp, H_DIFF, 2 * dh), dv_p.reshape(1, bp, sp, H_DIFF, dv_diff),
            dk_s.reshape(1, bs, ls, H_DIFF, 2 * dh), dv_s.reshape(1, bs, ls, H_DIFF, dv_diff),
            gv.reshape(1, bs, ls, -1))
```

```python
import functools
import math
from typing import NamedTuple

import jax
import jax.numpy as jnp
from jax import lax
from jax.experimental import pallas as pl
from jax.experimental.pallas import tpu as pltpu

F32 = jnp.float32
BF16 = jnp.bfloat16
EPS = 1e-6
ROPE_THETA = 10000.0
H_RET = 4
H_DIFF = 4
GMLP_GROUPS = 8
GMLP_CHUNK = 128
LANES = 128
VMEM_LIMIT_BYTES = 58 * 1024 * 1024
NEG_BIG = -1e30

ROW_TILE = 512
GMLP_TILE = 256
RET_CHUNK = 256
ATTN_TILE = 512
FF_CHUNK = 512
DEC_CHUNK_PAGES = 8
DEC_SLOTS = 4
DEC_SOFTMAX_PIECE = 8192


def _params(semantics):
    return pltpu.CompilerParams(dimension_semantics=semantics, vmem_limit_bytes=VMEM_LIMIT_BYTES)


def _resident(shape):
    return pl.BlockSpec(shape, lambda *_: (0,) * len(shape), pipeline_mode=pl.Buffered(1))


def _rms(x, g):
    return x * lax.rsqrt(jnp.mean(x * x, axis=-1, keepdims=True) + EPS) * g


def _log_gamma(h):
    return math.log1p(-(2.0 ** (-5.0 - h)))


def _in_proj_body(x_ref, g_ref, w_ref, cr_ref, sr_ref, cd_ref, sd_ref, qg_ref, kg_ref, grp_ref,
                  rq_ref, rk_ref, rv_ref, gate_ref, dq_ref, dk_ref, dv_ref, dkb_ref, dvb_ref,
                  *, ret_qk, ret_v, diff_qk, diff_v, dh):
    hb = _rms(x_ref[...], g_ref[...]).astype(BF16)
    rows = hb.shape[0]

    def proj(lo, width):
        return jnp.dot(hb, w_ref[:, lo:lo + width], preferred_element_type=F32)

    cr, sr = cr_ref[...], sr_ref[...]
    for col0, out_ref in ((0, rq_ref), (ret_qk, rk_ref)):
        p = proj(col0, ret_qk)
        for c in range(ret_qk // LANES):
            s = p[:, c * LANES:(c + 1) * LANES]
            out_ref[:, c * LANES:(c + 1) * LANES] = (s * cr + pltpu.roll(s, LANES // 2, 1) * sr).astype(BF16)
    rv_ref[...] = proj(2 * ret_qk, ret_v).astype(BF16)
    rg = proj(2 * ret_qk + ret_v, ret_v)
    gate_ref[...] = (rg * jax.nn.sigmoid(rg)).astype(BF16)

    cd, sd = cd_ref[...], sd_ref[...]
    lane = lax.broadcasted_iota(jnp.int32, (rows, LANES), 1)
    first_half = (lane & (dh // 2)) == 0
    grp = grp_ref[...]
    gw = grp.shape[0]
    base = 2 * ret_qk + 2 * ret_v

    def norm_rope(col0, gain):
        p = proj(col0, diff_qk)
        slabs = []
        for c in range(diff_qk // gw):
            s = p[:, c * gw:(c + 1) * gw]
            ssq = jnp.dot((s * s).astype(BF16), grp, preferred_element_type=F32)
            n = s * lax.rsqrt(ssq * (1.0 / dh) + EPS)
            for d in range(gw // LANES):
                t = n[:, d * LANES:(d + 1) * LANES] * gain
                partner = jnp.where(first_half, pltpu.roll(t, LANES - dh // 2, 1), pltpu.roll(t, dh // 2, 1))
                slabs.append(t * cd + partner * sd)
        return slabs

    q_scale = dh ** -0.5 * math.log2(math.e)
    for c, s in enumerate(norm_rope(base, qg_ref[...])):
        dq_ref[:, c * LANES:(c + 1) * LANES] = (s * q_scale).astype(BF16)
    for c, s in enumerate(norm_rope(base + diff_qk, kg_ref[...])):
        dk_ref[:, c, :] = s
        dkb_ref[:, c * LANES:(c + 1) * LANES] = s.astype(BF16)
    dv = proj(base + 2 * diff_qk, diff_v)
    for c in range(diff_v // LANES):
        dv_ref[:, c, :] = dv[:, c * LANES:(c + 1) * LANES]
    dvb_ref[...] = dv.astype(BF16)


def _in_proj(x, g, w, tabs, tab_blocks, qg, kg, grp, tm, dims):
    rows, d = x.shape
    ret_qk, ret_v, diff_qk, diff_v, dh = dims
    n = rows // tm
    row = lambda width: pl.BlockSpec((tm, width), lambda i: (i, 0))
    tab = pl.BlockSpec((tm, LANES), lambda i: (i % tab_blocks, 0))
    body = functools.partial(_in_proj_body, ret_qk=ret_qk, ret_v=ret_v, diff_qk=diff_qk, diff_v=diff_v, dh=dh)
    shapes = [((rows, ret_qk), BF16), ((rows, ret_qk), BF16), ((rows, ret_v), BF16), ((rows, ret_v), BF16),
              ((rows, diff_qk), BF16), ((rows, diff_qk // LANES, LANES), F32), ((rows, diff_v // LANES, LANES), F32),
              ((rows, diff_qk), BF16), ((rows, diff_v), BF16)]
    out_spec = lambda s: row(s[1]) if len(s) == 2 else pl.BlockSpec((tm,) + s[1:], lambda i: (i, 0, 0))
    return pl.pallas_call(
        body,
        grid=(n,),
        in_specs=[row(d), _resident(g.shape), _resident(w.shape), tab, tab, tab, tab,
                  _resident(qg.shape), _resident(kg.shape), _resident(grp.shape)],
        out_specs=[out_spec(s) for s, _ in shapes],
        out_shape=[jax.ShapeDtypeStruct(s, t) for s, t in shapes],
        compiler_params=_params(("parallel",)),
        name="in_proj",
    )(x, g, w, *tabs, qg, kg, grp)


def _retention_body(q_ref, k_ref, v_ref, gate_ref, ro_ref, sfin_ref, state, dmask, *, dk, dv):
    b, c = pl.program_id(0), pl.program_id(1)
    chunk = q_ref.shape[0]
    scale = dk ** -0.5

    @pl.when((b == 0) & (c == 0))
    def _():
        i = lax.broadcasted_iota(jnp.int32, (chunk, chunk), 0)
        j = lax.broadcasted_iota(jnp.int32, (chunk, chunk), 1)
        causal = i >= j
        diff = jnp.where(causal, i - j, 0).astype(F32)
        for h in range(H_RET):
            dmask[h] = jnp.where(causal, jnp.exp(diff * _log_gamma(h)), 0.0) * scale

    @pl.when(c == 0)
    def _():
        state[...] = jnp.zeros_like(state)

    idx = lax.broadcasted_iota(jnp.int32, (chunk, 1), 0).astype(F32)
    for h in range(H_RET):
        lg = _log_gamma(h)
        q = q_ref[:, h * dk:(h + 1) * dk]
        k = k_ref[:, h * dk:(h + 1) * dk]
        v = v_ref[:, h * dv:(h + 1) * dv]
        s_prev = state[h]
        scores = lax.dot_general(q, k, (((1,), (1,)), ((), ())), preferred_element_type=F32) * dmask[h]
        inner = jnp.dot(scores.astype(BF16), v, preferred_element_type=F32)
        cross = jnp.dot(q, s_prev.astype(BF16), preferred_element_type=F32) * jnp.exp((idx + 1.0) * lg)
        o = inner + cross
        zeta = jnp.exp((chunk - 1.0 - idx) * lg) * scale
        kz_t = (k.astype(F32) * zeta).T.astype(BF16)
        state[h] = math.exp(chunk * lg) * s_prev + jnp.dot(kz_t, v, preferred_element_type=F32)
        oc = o - jnp.mean(o, axis=-1, keepdims=True)
        on = oc * lax.rsqrt(jnp.mean(oc * oc, axis=-1, keepdims=True) + EPS)
        ro_ref[:, h * dv:(h + 1) * dv] = (on * gate_ref[:, h * dv:(h + 1) * dv].astype(F32)).astype(BF16)

    @pl.when(c == pl.num_programs(1) - 1)
    def _():
        sfin_ref[0] = state[...]


def _retention_prompt(rq, rk, rv, gate, batch, seq, dk, dv):
    chunk = RET_CHUNK if seq % RET_CHUNK == 0 else seq
    nc = seq // chunk
    blk = lambda width: pl.BlockSpec((chunk, width), lambda b, c: (b * nc + c, 0))
    return pl.pallas_call(
        functools.partial(_retention_body, dk=dk, dv=dv),
        grid=(batch, nc),
        in_specs=[blk(H_RET * dk), blk(H_RET * dk), blk(H_RET * dv), blk(H_RET * dv)],
        out_specs=[blk(H_RET * dv), pl.BlockSpec((1, H_RET, dk, dv), lambda b, c: (b, 0, 0, 0))],
        out_shape=[jax.ShapeDtypeStruct((batch * seq, H_RET * dv), BF16),
                   jax.ShapeDtypeStruct((batch, H_RET, dk, dv), F32)],
        scratch_shapes=[pltpu.VMEM((H_RET, dk, dv), F32), pltpu.VMEM((H_RET, chunk, chunk), F32)],
        compiler_params=_params(("arbitrary", "arbitrary")),
        name="retention_prompt",
    )(rq, rk, rv, gate)


def _retention_step_body(s_ref, q_ref, k_ref, v_ref, gate_ref, snew_ref, ro_ref, *, dk):
    scale = dk ** -0.5
    for h in range(H_RET):
        s_new = math.exp(_log_gamma(h)) * s_ref[:, h] + (k_ref[:, h] * scale) * v_ref[:, h]
        snew_ref[:, h] = s_new
        o = jnp.sum(q_ref[:, h] * s_new, axis=1, keepdims=True)
        oc = o - jnp.mean(o, axis=-1, keepdims=True)
        on = oc * lax.rsqrt(jnp.mean(oc * oc, axis=-1, keepdims=True) + EPS)
        ro_ref[:, h] = (on * gate_ref[:, h].astype(F32)).astype(BF16)


def _retention_step(state, q_col, k_col, v_row, gate_row, bt):
    nb, _, dk, dv = state.shape
    col = pl.BlockSpec((bt, H_RET, dk, 1), lambda i: (i, 0, 0, 0))
    rowb = pl.BlockSpec((bt, H_RET, 1, dv), lambda i: (i, 0, 0, 0))
    st = pl.BlockSpec((bt, H_RET, dk, dv), lambda i: (i, 0, 0, 0))
    return pl.pallas_call(
        functools.partial(_retention_step_body, dk=dk),
        grid=(nb // bt,),
        in_specs=[st, col, col, rowb, rowb],
        out_specs=[st, rowb],
        out_shape=[jax.ShapeDtypeStruct(state.shape, F32), jax.ShapeDtypeStruct((nb, H_RET, 1, dv), BF16)],
        compiler_params=_params(("parallel",)),
        name="retention_step",
    )(state, q_col, k_col, v_row, gate_row)


def _lambda(lq1_ref, lk1_ref, lq2_ref, lk2_ref, lam_init):
    a = jnp.sum(lq1_ref[...] * lk1_ref[...], axis=-1, keepdims=True)
    b = jnp.sum(lq2_ref[...] * lk2_ref[...], axis=-1, keepdims=True)
    return jnp.exp(a) - jnp.exp(b) + lam_init


def _sub_norm(o, subg, lam_init):
    return o * lax.rsqrt(jnp.mean(o * o, axis=-1, keepdims=True) + EPS) * subg * (1.0 - lam_init)


def _attn_prompt_body(lq1_ref, lk1_ref, lq2_ref, lk2_ref, subg_ref, q_ref, k_ref, v_ref, o_ref, acc_ref, s_ref,
                      *, dh, lam_init):
    i = pl.program_id(2)
    t = q_ref.shape[0]
    dv = v_ref.shape[1]
    q = q_ref[...]
    lane = lax.broadcasted_iota(jnp.int32, q.shape, 1)
    zero = jnp.zeros_like(q)
    q2 = jnp.concatenate([jnp.where(lane < dh, q, zero), jnp.where(lane >= dh, q, zero)], axis=0)
    row = lax.broadcasted_iota(jnp.int32, (2 * t, t), 0)
    col = lax.broadcasted_iota(jnp.int32, (2 * t, t), 1)
    on_or_below = col <= jnp.where(row >= t, row - t, row)
    ones = jnp.ones((t, LANES), BF16)

    def scores(j):
        kb = k_ref[pl.ds(pl.multiple_of(j * t, t), t), :]
        return lax.dot_general(q2, kb, (((1,), (1,)), ((), ())), preferred_element_type=F32)

    def lane_max(s, m):
        for c in range(t // LANES):
            m = jnp.maximum(m, s[:, c * LANES:(c + 1) * LANES])
        return m

    def accumulate(j, s, mrow):
        p = jnp.exp2(s - mrow).astype(BF16)
        v1 = jnp.concatenate([v_ref[pl.ds(pl.multiple_of(j * t, t), t), :], ones], axis=1)
        acc_ref[...] += jnp.dot(p, v1, preferred_element_type=F32)

    def key_cols(j):
        return pl.ds(pl.multiple_of(j * t, t), t)

    def first_pass(j, m):
        s = scores(j)
        s_ref[:, key_cols(j)] = s
        return lane_max(s, m)

    s_diag = jnp.where(on_or_below, scores(i), NEG_BIG)
    m = lax.fori_loop(0, i, first_pass, lane_max(s_diag, jnp.full((2 * t, LANES), NEG_BIG, F32)))
    mrow = jnp.max(m, axis=-1, keepdims=True)
    acc_ref[...] = jnp.zeros_like(acc_ref)
    accumulate(i, s_diag, mrow)

    def second_pass(j, carry):
        accumulate(j, s_ref[:, key_cols(j)], mrow)
        return carry

    lax.fori_loop(0, i, second_pass, 0)
    acc = acc_ref[...]
    o_all = acc[:, :dv] / acc[:, dv:]
    lam = _lambda(lq1_ref, lk1_ref, lq2_ref, lk2_ref, lam_init)
    o = o_all[:t] - lam * o_all[t:]
    o_ref[...] = _sub_norm(o, subg_ref[...], lam_init).astype(BF16)


def _attn_prompt(dq, dkb, dvb, lam_vecs, subg, batch, seq, dh, lam_init):
    t = ATTN_TILE if seq % ATTN_TILE == 0 else seq
    nq = seq // t
    width = 2 * dh
    vec = [_resident(v.shape) for v in lam_vecs]
    return pl.pallas_call(
        functools.partial(_attn_prompt_body, dh=dh, lam_init=lam_init),
        grid=(batch, H_DIFF, nq),
        in_specs=vec + [_resident(subg.shape),
                        pl.BlockSpec((t, width), lambda b, h, i: (b * nq + i, h)),
                        pl.BlockSpec((seq, width), lambda b, h, i: (b, h)),
                        pl.BlockSpec((seq, width), lambda b, h, i: (b, h))],
        out_specs=pl.BlockSpec((t, width), lambda b, h, i: (b * nq + i, h)),
        out_shape=jax.ShapeDtypeStruct((batch * seq, H_DIFF * width), BF16),
        scratch_shapes=[pltpu.VMEM((2 * t, 2 * width), F32), pltpu.VMEM((2 * t, seq), F32)],
        compiler_params=_params(("parallel", "parallel", "arbitrary")),
        name="diff_attn_prompt",
    )(*lam_vecs, subg, dq, dkb, dvb)


class _DecodeSide(NamedTuple):
    first_sample: int
    n_samples: int
    lam_init: float


def _decode_softmax(probs, q8, knew, n_cols):
    piece = DEC_SOFTMAX_PIECE if n_cols % DEC_SOFTMAX_PIECE == 0 else n_cols
    lane = lax.broadcasted_iota(jnp.int32, (2 * H_DIFF, piece), 1)
    row = lax.broadcasted_iota(jnp.int32, (2 * H_DIFF, piece), 0)
    own_head = (lane & (H_DIFF - 1)) == (row & (H_DIFF - 1))
    s_new = jnp.sum(q8.astype(F32) * knew, axis=-1, keepdims=True)
    mx = s_new
    for c in range(n_cols // piece):
        s = jnp.where(own_head, probs[:, c * piece:(c + 1) * piece], NEG_BIG)
        mx = jnp.maximum(mx, jnp.max(s, axis=-1, keepdims=True))
    e_new = jnp.exp2(s_new - mx)
    total = e_new
    for c in range(n_cols // piece):
        e = jnp.exp2(jnp.where(own_head, probs[:, c * piece:(c + 1) * piece], NEG_BIG) - mx)
        probs[:, c * piece:(c + 1) * piece] = e
        total = total + jnp.sum(e, axis=-1, keepdims=True)
    return e_new, 1.0 / total


def _mix_ffn_body(*refs, widths, projected, side):
    if side is not None:
        pt_ref, refs = refs[0], refs[1:]
    n = len(widths)
    x_ref = refs[0]
    a_refs = refs[1:1 + n]
    if projected:
        wo_ref = None
        g_ref, wup_ref, wdn_ref = refs[1 + n:4 + n]
        rest = refs[4 + n:]
    else:
        wo_ref, g_ref, wup_ref, wdn_ref = refs[1 + n:5 + n]
        rest = refs[5 + n:]
    d_ff = wup_ref.shape[1]
    fc = FF_CHUNK if d_ff % FF_CHUNK == 0 else d_ff
    n_ff = d_ff // fc

    def ffn(after_chunk):
        a = a_refs[0][...] if n == 1 else jnp.concatenate([r[...] for r in a_refs], axis=-1)
        y = x_ref[...] + (a if projected else jnp.dot(a, wo_ref[...], preferred_element_type=F32))
        hb = _rms(y, g_ref[...]).astype(BF16)
        for c in range(n_ff):
            hid = jnp.dot(hb, wup_ref[:, c * fc:(c + 1) * fc], preferred_element_type=F32)
            act = jnp.square(jnp.maximum(hid, 0.0)).astype(BF16)
            y = y + jnp.dot(act, wdn_ref[c * fc:(c + 1) * fc, :], preferred_element_type=F32)
            after_chunk(c)
        y_ref[...] = y

    if side is None:
        (y_ref,) = rest
        ffn(lambda c: None)
        return

    (lq1_ref, lk1_ref, lq2_ref, lk2_ref, subg_ref, q8_ref, knew_ref, vnew_ref, ck_ref, cv_ref,
     y_ref, do_ref, ring, sems, probs, acc) = rest
    step = pl.program_id(0)
    n_steps = pl.num_programs(0)
    slots, cp, page_rows, _ = ring.shape
    n_pages = pt_ref.shape[1]
    n_chunks = n_pages // cp
    per_ff = n_chunks // n_ff
    look = slots
    b = step // 2

    def page_copy(phase, at_step, chunk, i):
        sample = side.first_sample + at_step // 2
        src = ck_ref if phase == 0 else cv_ref
        slot = chunk % slots
        return pltpu.make_async_copy(src.at[pt_ref[sample, chunk * cp + i]], ring.at[slot, i], sems.at[slot])

    def start_chunk(phase, at_step, chunk):
        for i in range(cp):
            page_copy(phase, at_step, chunk, i).start()

    def use_chunk(phase, chunk):
        cols = slice(chunk * cp * page_rows, (chunk + 1) * cp * page_rows)
        pages = ring[chunk % slots].reshape(cp * page_rows, ring.shape[-1]).astype(BF16)
        if phase == 0:
            probs[:, cols] = lax.dot_general(q8_ref[b], pages, (((1,), (1,)), ((), ())), preferred_element_type=F32)
        else:
            acc[...] += jnp.dot(probs[:, cols].astype(BF16), pages, preferred_element_type=F32)

    def after_chunk(phase, c):
        chunks = range(c * per_ff, (c + 1) * per_ff)
        for chunk in chunks:
            for i in range(cp):
                page_copy(phase, step, chunk, i).wait()
        for chunk in chunks:
            use_chunk(phase, chunk)
        for chunk in chunks:
            ahead = chunk + look
            if ahead < n_chunks:
                start_chunk(phase, step, ahead)
            else:
                @pl.when(step + 1 < n_steps)
                def _():
                    start_chunk(1 - phase, step + 1, ahead - n_chunks)

    @pl.when(step == 0)
    def _():
        for chunk in range(look):
            start_chunk(0, step, chunk)

    @pl.when(step % 2 == 0)
    def _():
        ffn(functools.partial(after_chunk, 0))

    @pl.when(step % 2 == 1)
    def _():
        e_new, inv = _decode_softmax(probs, q8_ref[b], knew_ref[b], n_pages * page_rows)
        acc[...] = jnp.zeros_like(acc)
        ffn(functools.partial(after_chunk, 1))
        vnew = vnew_ref[b]
        a = (acc[...] + e_new * jnp.concatenate([vnew, vnew], axis=0)) * inv
        lam = _lambda(lq1_ref, lk1_ref, lq2_ref, lk2_ref, side.lam_init)
        o = a[:H_DIFF] - lam * a[H_DIFF:]
        do_ref[b] = _sub_norm(o, subg_ref[...], side.lam_init)


def _mix_ffn(x, parts, w_o, g, w_up, w_dn, tm, side=None, decode_ops=None):
    rows, d = x.shape
    widths = tuple(p.shape[1] for p in parts)
    n_steps = rows // tm
    row = lambda width: pl.BlockSpec((tm, width), lambda i, *_: (i, 0))
    weights = ([] if w_o is None else [w_o]) + [g, w_up, w_dn]
    in_specs = [row(d)] + [row(w) for w in widths] + [_resident(w.shape) for w in weights]
    body = functools.partial(_mix_ffn_body, widths=widths, projected=w_o is None, side=side)
    if side is None:
        return pl.pallas_call(
            body, grid=(n_steps,), in_specs=in_specs, out_specs=row(d),
            out_shape=jax.ShapeDtypeStruct((rows, d), F32),
            compiler_params=_params(("parallel",)), name="mix_ffn",
        )(x, *parts, *weights)

    page_table, cache_k, cache_v, q8, knew8, vnew, lam_vecs, subg = decode_ops
    n_pages = page_table.shape[1]
    _, page_rows, width = cache_k.shape
    assert n_steps == 2 * side.n_samples and n_pages % DEC_CHUNK_PAGES == 0
    assert (n_pages // DEC_CHUNK_PAGES) % (w_up.shape[1] // FF_CHUNK) == 0
    assert (n_pages // DEC_CHUNK_PAGES) % DEC_SLOTS == 0
    lo, hi = side.first_sample, side.first_sample + side.n_samples
    local = [q8[lo:hi], knew8[lo:hi], vnew[lo:hi]]
    any_spec = pl.BlockSpec(memory_space=pl.ANY)
    grid_spec = pltpu.PrefetchScalarGridSpec(
        num_scalar_prefetch=1,
        grid=(n_steps,),
        in_specs=in_specs + [_resident(v.shape) for v in lam_vecs] + [_resident(subg.shape)]
        + [_resident(a.shape) for a in local] + [any_spec, any_spec],
        out_specs=[row(d), pl.BlockSpec((side.n_samples, H_DIFF, width), lambda *_: (0, 0, 0))],
        scratch_shapes=[pltpu.VMEM((DEC_SLOTS, DEC_CHUNK_PAGES, page_rows, width), F32),
                        pltpu.SemaphoreType.DMA((DEC_SLOTS,)),
                        pltpu.VMEM((2 * H_DIFF, n_pages * page_rows), F32),
                        pltpu.VMEM((2 * H_DIFF, width), F32)],
    )
    return pl.pallas_call(
        body, grid_spec=grid_spec,
        out_shape=[jax.ShapeDtypeStruct((rows, d), F32), jax.ShapeDtypeStruct((side.n_samples, H_DIFF, width), F32)],
        compiler_params=_params(("arbitrary",)), name="mix_ffn_decode",
    )(page_table, x, *parts, *weights, *lam_vecs, subg, *local, cache_k, cache_v)


def _gelu(x):
    return 0.5 * x * (1.0 + lax.erf(x * (2.0 ** -0.5)))


def _gmlp_body(x_ref, g_ref, w_ref, lng_ref, lnb_ref, ws_ref, bs_ref, wout_ref, mix_ref, *rest, e, single_token):
    if single_token:
        vn_ref, vbuf, a_ref = rest
    else:
        vbuf, a_ref = rest
    hb = _rms(x_ref[...], g_ref[...]).astype(BF16)
    rows = hb.shape[0]
    gc = e // GMLP_GROUPS
    pc = 2 * gc
    for pair in range(GMLP_GROUPS // 2):
        vbuf[:, pair * pc:(pair + 1) * pc] = _gelu(
            jnp.dot(hb, w_ref[:, e + pair * pc:e + (pair + 1) * pc], preferred_element_type=F32))
    v = vbuf[...]
    vc = v - jnp.mean(v, axis=-1, keepdims=True)
    vn = vc * lax.rsqrt(jnp.mean(vc * vc, axis=-1, keepdims=True) + EPS) * lng_ref[...] + lnb_ref[...]
    vbuf[...] = vn
    if single_token:
        vn_ref[...] = vn
    else:
        t = lax.broadcasted_iota(jnp.int32, (GMLP_CHUNK, GMLP_CHUNK), 0)
        s_ = lax.broadcasted_iota(jnp.int32, (GMLP_CHUNK, GMLP_CHUNK), 1)
        tril = t >= s_
    for grp in range(GMLP_GROUPS):
        cols = slice(grp * gc, (grp + 1) * gc)
        if grp % 2 == 0:
            u2 = _gelu(jnp.dot(hb, w_ref[:, grp * gc:grp * gc + pc], preferred_element_type=F32))
        u = u2[:, (grp % 2) * gc:(grp % 2 + 1) * gc]
        if single_token:
            a_ref[:, cols] = (u * (vbuf[:, cols] * ws_ref[grp][0:1, 0:1] + bs_ref[0:1, grp:grp + 1])).astype(BF16)
        else:
            wsg = jnp.where(tril, ws_ref[grp], 0.0).astype(BF16)
            bias = bs_ref[:, grp:grp + 1]
            for r in range(rows // GMLP_CHUNK):
                rsl = slice(r * GMLP_CHUNK, (r + 1) * GMLP_CHUNK)
                s = jnp.dot(wsg, vbuf[rsl, cols].astype(BF16), preferred_element_type=F32) + bias
                a_ref[rsl, cols] = (u[rsl] * s).astype(BF16)
    mix_ref[...] = jnp.dot(a_ref[...], wout_ref[...], preferred_element_type=F32)


def _gmlp(x, g, w, lng, lnb, ws, bs_t, w_out, tm, single_token):
    rows, d = x.shape
    e = w.shape[1] // 2
    row = lambda width: pl.BlockSpec((tm, width), lambda i: (i, 0))
    out_specs = [row(d)]
    out_shape = [jax.ShapeDtypeStruct((rows, d), F32)]
    if single_token:
        out_specs.append(row(e))
        out_shape.append(jax.ShapeDtypeStruct((rows, e), F32))
    return pl.pallas_call(
        functools.partial(_gmlp_body, e=e, single_token=single_token),
        grid=(rows // tm,),
        in_specs=[row(d), _resident(g.shape), _resident(w.shape), _resident(lng.shape), _resident(lnb.shape),
                  _resident(ws.shape), _resident(bs_t.shape), _resident(w_out.shape)],
        out_specs=out_specs,
        out_shape=out_shape,
        scratch_shapes=[pltpu.VMEM((tm, e), F32), pltpu.VMEM((tm, e), BF16)],
        compiler_params=_params(("parallel",)),
        name="gmlp_sample" if single_token else "gmlp_prompt",
    )(x, g, w, lng, lnb, ws, bs_t, w_out)


def _rope_tables(pos, d):
    inv = 1.0 / (ROPE_THETA ** (jnp.arange(0, d, 2, dtype=F32) / d))
    ang = pos.astype(F32)[:, None] * inv[None, :]
    c, s = jnp.cos(ang), jnp.sin(ang)
    reps = LANES // d
    return jnp.tile(jnp.concatenate([c, c], -1), (1, reps)), jnp.tile(jnp.concatenate([-s, s], -1), (1, reps))


def kernel(x_prompt, x_sample, state_ret, cache_k, cache_v, page_table, rms_mix_g, rms_ffn_g, w_ffn_up, w_ffn_down,
           w_in_even, w_out_even, q_norm_g, k_norm_g, lam_q1, lam_k1, lam_q2, lam_k2, subln_g,
           w_in_odd, ln_v_g, ln_v_b, w_s, b_s, w_out_odd):
    bp, sp, d = x_prompt.shape
    bs, ls, _ = x_sample.shape
    depth = rms_mix_g.shape[0]
    dk_ret, dv_ret = state_ret.shape[-2:]
    n_pool, page_size = cache_k.shape[1:3]
    dh = cache_k.shape[-1] // 2
    dv_diff = cache_v.shape[-1]
    assert dk_ret == LANES and 2 * dh == LANES and dv_diff == LANES and ls == 1
    ret_qk, ret_v = H_RET * dk_ret, H_RET * dv_ret
    diff_qk, diff_v = H_DIFF * 2 * dh, H_DIFF * dv_diff
    dims = (ret_qk, ret_v, diff_qk, diff_v, dh)
    past = page_table.shape[1] * page_size
    tm = ROW_TILE if (bp * sp) % ROW_TILE == 0 and sp % ROW_TILE == 0 else sp
    rows_s = bs * ls

    pos_p = jnp.arange(sp, dtype=jnp.int32)
    pos_s = jnp.tile(past + jnp.arange(ls, dtype=jnp.int32), bs)
    tabs_p = _rope_tables(pos_p, dk_ret) + _rope_tables(pos_p, dh)
    tabs_s = _rope_tables(pos_s, dk_ret) + _rope_tables(pos_s, dh)
    gidx = jnp.arange(2 * LANES) // dh
    grp = (gidx[:, None] == gidx[None, :]).astype(BF16)

    assert depth == 2, "the decode attention of layer 0 is spread over the prompt MLP calls of layers 0 and 1"
    yp = x_prompt.reshape(bp * sp, d)
    ys = x_sample.reshape(rows_s, d)
    ffn_w = [(rms_ffn_g[l][None], w_ffn_up[l].astype(BF16), w_ffn_down[l].astype(BF16)) for l in range(depth)]

    lam_init = 0.8 - 0.6 * math.exp(-0.3 * 0)
    g_mix = rms_mix_g[0][None]
    w_in = w_in_even[0].astype(BF16)
    w_out = w_out_even[0].astype(BF16)
    qg = jnp.tile(q_norm_g[0], LANES // dh)[None]
    kg = jnp.tile(k_norm_g[0], LANES // dh)[None]
    lam_vecs = [lam_q1[0][None], lam_k1[0][None], lam_q2[0][None], lam_k2[0][None]]
    subg = subln_g[0][None]

    rq, rk, rv, gate, dq, dk_p, dv_p, dkb, dvb = _in_proj(yp, g_mix, w_in, tabs_p, sp // tm, qg, kg, grp, tm, dims)
    ro_p, s_fin = _retention_prompt(rq, rk, rv, gate, bp, sp, dk_ret, dv_ret)
    do_p = _attn_prompt(dq, dkb, dvb, lam_vecs, subg, bp, sp, dh, lam_init)

    rq, rk, rv, gate, dq, dk_s, dv_s, _, _ = _in_proj(ys, g_mix, w_in, tabs_s, 1, qg, kg, grp, rows_s, dims)
    s_new, ro_s = _retention_step(
        state_ret[0],
        rq.astype(F32).reshape(bs, H_RET, dk_ret, 1), rk.astype(F32).reshape(bs, H_RET, dk_ret, 1),
        rv.reshape(bs, H_RET, 1, dv_ret), gate.reshape(bs, H_RET, 1, dv_ret), 4 if bs % 4 == 0 else 1)
    q4 = dq.reshape(bs, H_DIFF, 2 * dh)
    half = (jnp.arange(2 * dh) < dh)[None, None, :]
    q8 = jnp.concatenate([jnp.where(half, q4, 0), jnp.where(half, 0, q4)], axis=1)
    k4 = dk_s
    decode_ops = (page_table,
                  cache_k[0].reshape(n_pool, page_size * H_DIFF, 2 * dh),
                  cache_v[0].reshape(n_pool, page_size * H_DIFF, dv_diff),
                  q8, jnp.concatenate([k4, k4], axis=1), dv_s.reshape(bs, H_DIFF, dv_diff), lam_vecs, subg)
    n_first = bs // 2

    yp, do_a = _mix_ffn(yp, [ro_p, do_p], w_out, *ffn_w[0], tm, _DecodeSide(0, n_first, lam_init), decode_ops)

    g_mix1 = rms_mix_g[1][None]
    w_in1 = w_in_odd[0].astype(BF16)
    w_out1 = w_out_odd[0].astype(BF16)
    lng, lnb = ln_v_g[0][None], ln_v_b[0][None]
    bs_t = b_s[0].T
    (mix,) = _gmlp(yp, g_mix1, w_in1, lng, lnb, w_s[0], bs_t, w_out1, min(tm, GMLP_TILE), False)
    yp, do_b = _mix_ffn(yp, [mix], None, *ffn_w[1], tm, _DecodeSide(n_first, bs - n_first, lam_init), decode_ops)

    do_s = jnp.concatenate([do_a, do_b], axis=0).astype(BF16)
    ys = _mix_ffn(ys, [ro_s.reshape(rows_s, ret_v), do_s.reshape(rows_s, diff_v)], w_out, *ffn_w[0], rows_s)
    mix, gv = _gmlp(ys, g_mix1, w_in1, lng, lnb, w_s[0], bs_t, w_out1, rows_s, True)
    ys = _mix_ffn(ys, [mix], None, *ffn_w[1], rows_s)

    return (yp.reshape(bp, sp, d), ys.reshape(bs, ls, d), s_fin[None], s_new[None],
            dk_p.reshape(1, bp, sp, H_DIFF, 2 * dh), dv_p.reshape(1, bp, sp, H_DIFF, dv_diff),
            dk_s.reshape(1, bs, ls, H_DIFF, 2 * dh), dv_s.reshape(1, bs, ls, H_DIFF, dv_diff),
            gv.reshape(1, bs, ls, -1))
```

```python
import functools
import math
from typing import NamedTuple

import jax
import jax.numpy as jnp
from jax import lax
from jax.experimental import pallas as pl
from jax.experimental.pallas import tpu as pltpu

F32 = jnp.float32
BF16 = jnp.bfloat16
EPS = 1e-6
ROPE_THETA = 10000.0
H_RET = 4
H_DIFF = 4
GMLP_GROUPS = 8
GMLP_CHUNK = 128
LANES = 128
VMEM_LIMIT_BYTES = 58 * 1024 * 1024
NEG_BIG = -1e30

ROW_TILE = 512
GMLP_TILE = 512
RET_CHUNK = 256
ATTN_TILE = 512
FF_CHUNK = 512
DEC_CHUNK_PAGES = 8
DEC_SLOTS = 4
DEC_SOFTMAX_PIECE = 8192


def _params(semantics):
    return pltpu.CompilerParams(dimension_semantics=semantics, vmem_limit_bytes=VMEM_LIMIT_BYTES)


def _resident(shape):
    return pl.BlockSpec(shape, lambda *_: (0,) * len(shape), pipeline_mode=pl.Buffered(1))


def _rms(x, g):
    return x * lax.rsqrt(jnp.mean(x * x, axis=-1, keepdims=True) + EPS) * g


def _log_gamma(h):
    return math.log1p(-(2.0 ** (-5.0 - h)))


def _in_proj_body(x_ref, g_ref, w_ref, cr_ref, sr_ref, cd_ref, sd_ref, qg_ref, kg_ref, grp_ref,
                  rq_ref, rk_ref, rv_ref, gate_ref, dq_ref, dk_ref, dv_ref, dkb_ref, dvb_ref,
                  *, ret_qk, ret_v, diff_qk, diff_v, dh):
    hb = _rms(x_ref[...], g_ref[...]).astype(BF16)
    rows = hb.shape[0]

    def proj(lo, width):
        return jnp.dot(hb, w_ref[:, lo:lo + width], preferred_element_type=F32)

    cr, sr = cr_ref[...], sr_ref[...]
    for col0, out_ref in ((0, rq_ref), (ret_qk, rk_ref)):
        p = proj(col0, ret_qk)
        for c in range(ret_qk // LANES):
            s = p[:, c * LANES:(c + 1) * LANES]
            out_ref[:, c * LANES:(c + 1) * LANES] = (s * cr + pltpu.roll(s, LANES // 2, 1) * sr).astype(BF16)
    rv_ref[...] = proj(2 * ret_qk, ret_v).astype(BF16)
    rg = proj(2 * ret_qk + ret_v, ret_v)
    gate_ref[...] = (rg * jax.nn.sigmoid(rg)).astype(BF16)

    cd, sd = cd_ref[...], sd_ref[...]
    lane = lax.broadcasted_iota(jnp.int32, (rows, LANES), 1)
    first_half = (lane & (dh // 2)) == 0
    grp = grp_ref[...]
    gw = grp.shape[0]
    base = 2 * ret_qk + 2 * ret_v

    def norm_rope(col0, gain):
        p = proj(col0, diff_qk)
        slabs = []
        for c in range(diff_qk // gw):
            s = p[:, c * gw:(c + 1) * gw]
            ssq = jnp.dot((s * s).astype(BF16), grp, preferred_element_type=F32)
            n = s * lax.rsqrt(ssq * (1.0 / dh) + EPS)
            for d in range(gw // LANES):
                t = n[:, d * LANES:(d + 1) * LANES] * gain
                partner = jnp.where(first_half, pltpu.roll(t, LANES - dh // 2, 1), pltpu.roll(t, dh // 2, 1))
                slabs.append(t * cd + partner * sd)
        return slabs

    q_scale = dh ** -0.5 * math.log2(math.e)
    for c, s in enumerate(norm_rope(base, qg_ref[...])):
        dq_ref[:, c * LANES:(c + 1) * LANES] = (s * q_scale).astype(BF16)
    for c, s in enumerate(norm_rope(base + diff_qk, kg_ref[...])):
        dk_ref[:, c, :] = s
        dkb_ref[:, c * LANES:(c + 1) * LANES] = s.astype(BF16)
    dv = proj(base + 2 * diff_qk, diff_v)
    for c in range(diff_v // LANES):
        dv_ref[:, c, :] = dv[:, c * LANES:(c + 1) * LANES]
    dvb_ref[...] = dv.astype(BF16)


def _in_proj(x, g, w, tabs, tab_blocks, qg, kg, grp, tm, dims):
    rows, d = x.shape
    ret_qk, ret_v, diff_qk, diff_v, dh = dims
    n = rows // tm
    row = lambda width: pl.BlockSpec((tm, width), lambda i: (i, 0))
    tab = pl.BlockSpec((tm, LANES), lambda i: (i % tab_blocks, 0))
    body = functools.partial(_in_proj_body, ret_qk=ret_qk, ret_v=ret_v, diff_qk=diff_qk, diff_v=diff_v, dh=dh)
    shapes = [((rows, ret_qk), BF16), ((rows, ret_qk), BF16), ((rows, ret_v), BF16), ((rows, ret_v), BF16),
              ((rows, diff_qk), BF16), ((rows, diff_qk // LANES, LANES), F32), ((rows, diff_v // LANES, LANES), F32),
              ((rows, diff_qk), BF16), ((rows, diff_v), BF16)]
    out_spec = lambda s: row(s[1]) if len(s) == 2 else pl.BlockSpec((tm,) + s[1:], lambda i: (i, 0, 0))
    return pl.pallas_call(
        body,
        grid=(n,),
        in_specs=[row(d), _resident(g.shape), _resident(w.shape), tab, tab, tab, tab,
                  _resident(qg.shape), _resident(kg.shape), _resident(grp.shape)],
        out_specs=[out_spec(s) for s, _ in shapes],
        out_shape=[jax.ShapeDtypeStruct(s, t) for s, t in shapes],
        compiler_params=_params(("parallel",)),
        name="in_proj",
    )(x, g, w, *tabs, qg, kg, grp)


def _retention_body(q_ref, k_ref, v_ref, gate_ref, ro_ref, sfin_ref, state, dmask, *, dk, dv):
    b, c = pl.program_id(0), pl.program_id(1)
    chunk = q_ref.shape[0]
    scale = dk ** -0.5

    @pl.when((b == 0) & (c == 0))
    def _():
        i = lax.broadcasted_iota(jnp.int32, (chunk, chunk), 0)
        j = lax.broadcasted_iota(jnp.int32, (chunk, chunk), 1)
        causal = i >= j
        diff = jnp.where(causal, i - j, 0).astype(F32)
        for h in range(H_RET):
            dmask[h] = jnp.where(causal, jnp.exp(diff * _log_gamma(h)), 0.0) * scale

    @pl.when(c == 0)
    def _():
        state[...] = jnp.zeros_like(state)

    idx = lax.broadcasted_iota(jnp.int32, (chunk, 1), 0).astype(F32)
    for h in range(H_RET):
        lg = _log_gamma(h)
        q = q_ref[:, h * dk:(h + 1) * dk]
        k = k_ref[:, h * dk:(h + 1) * dk]
        v = v_ref[:, h * dv:(h + 1) * dv]
        s_prev = state[h]
        scores = lax.dot_general(q, k, (((1,), (1,)), ((), ())), preferred_element_type=F32) * dmask[h]
        inner = jnp.dot(scores.astype(BF16), v, preferred_element_type=F32)
        cross = jnp.dot(q, s_prev.astype(BF16), preferred_element_type=F32) * jnp.exp((idx + 1.0) * lg)
        o = inner + cross
        zeta = jnp.exp((chunk - 1.0 - idx) * lg) * scale
        kz_t = (k.astype(F32) * zeta).T.astype(BF16)
        state[h] = math.exp(chunk * lg) * s_prev + jnp.dot(kz_t, v, preferred_element_type=F32)
        oc = o - jnp.mean(o, axis=-1, keepdims=True)
        on = oc * lax.rsqrt(jnp.mean(oc * oc, axis=-1, keepdims=True) + EPS)
        ro_ref[:, h * dv:(h + 1) * dv] = (on * gate_ref[:, h * dv:(h + 1) * dv].astype(F32)).astype(BF16)

    @pl.when(c == pl.num_programs(1) - 1)
    def _():
        sfin_ref[0] = state[...]


def _retention_prompt(rq, rk, rv, gate, batch, seq, dk, dv):
    chunk = RET_CHUNK if seq % RET_CHUNK == 0 else seq
    nc = seq // chunk
    blk = lambda width: pl.BlockSpec((chunk, width), lambda b, c: (b * nc + c, 0))
    return pl.pallas_call(
        functools.partial(_retention_body, dk=dk, dv=dv),
        grid=(batch, nc),
        in_specs=[blk(H_RET * dk), blk(H_RET * dk), blk(H_RET * dv), blk(H_RET * dv)],
        out_specs=[blk(H_RET * dv), pl.BlockSpec((1, H_RET, dk, dv), lambda b, c: (b, 0, 0, 0))],
        out_shape=[jax.ShapeDtypeStruct((batch * seq, H_RET * dv), BF16),
                   jax.ShapeDtypeStruct((batch, H_RET, dk, dv), F32)],
        scratch_shapes=[pltpu.VMEM((H_RET, dk, dv), F32), pltpu.VMEM((H_RET, chunk, chunk), F32)],
        compiler_params=_params(("arbitrary", "arbitrary")),
        name="retention_prompt",
    )(rq, rk, rv, gate)


def _retention_step_body(s_ref, q_ref, k_ref, v_ref, gate_ref, snew_ref, ro_ref, *, dk):
    scale = dk ** -0.5
    for h in range(H_RET):
        s_new = math.exp(_log_gamma(h)) * s_ref[:, h] + (k_ref[:, h] * scale) * v_ref[:, h]
        snew_ref[:, h] = s_new
        o = jnp.sum(q_ref[:, h] * s_new, axis=1, keepdims=True)
        oc = o - jnp.mean(o, axis=-1, keepdims=True)
        on = oc * lax.rsqrt(jnp.mean(oc * oc, axis=-1, keepdims=True) + EPS)
        ro_ref[:, h] = (on * gate_ref[:, h].astype(F32)).astype(BF16)


def _retention_step(state, q_col, k_col, v_row, gate_row, bt):
    nb, _, dk, dv = state.shape
    col = pl.BlockSpec((bt, H_RET, dk, 1), lambda i: (i, 0, 0, 0))
    rowb = pl.BlockSpec((bt, H_RET, 1, dv), lambda i: (i, 0, 0, 0))
    st = pl.BlockSpec((bt, H_RET, dk, dv), lambda i: (i, 0, 0, 0))
    return pl.pallas_call(
        functools.partial(_retention_step_body, dk=dk),
        grid=(nb // bt,),
        in_specs=[st, col, col, rowb, rowb],
        out_specs=[st, rowb],
        out_shape=[jax.ShapeDtypeStruct(state.shape, F32), jax.ShapeDtypeStruct((nb, H_RET, 1, dv), BF16)],
        compiler_params=_params(("parallel",)),
        name="retention_step",
    )(state, q_col, k_col, v_row, gate_row)


def _lambda(lq1_ref, lk1_ref, lq2_ref, lk2_ref, lam_init):
    a = jnp.sum(lq1_ref[...] * lk1_ref[...], axis=-1, keepdims=True)
    b = jnp.sum(lq2_ref[...] * lk2_ref[...], axis=-1, keepdims=True)
    return jnp.exp(a) - jnp.exp(b) + lam_init


def _sub_norm(o, subg, lam_init):
    return o * lax.rsqrt(jnp.mean(o * o, axis=-1, keepdims=True) + EPS) * subg * (1.0 - lam_init)


def _attn_prompt_body(lq1_ref, lk1_ref, lq2_ref, lk2_ref, subg_ref, q_ref, k_ref, v_ref, o_ref, acc_ref, s_ref,
                      *, dh, lam_init):
    i = pl.program_id(2)
    t = q_ref.shape[0]
    dv = v_ref.shape[1]
    q = q_ref[...]
    lane = lax.broadcasted_iota(jnp.int32, q.shape, 1)
    zero = jnp.zeros_like(q)
    q2 = jnp.concatenate([jnp.where(lane < dh, q, zero), jnp.where(lane >= dh, q, zero)], axis=0)
    row = lax.broadcasted_iota(jnp.int32, (2 * t, t), 0)
    col = lax.broadcasted_iota(jnp.int32, (2 * t, t), 1)
    on_or_below = col <= jnp.where(row >= t, row - t, row)
    ones = jnp.ones((t, LANES), BF16)

    def scores(j):
        kb = k_ref[pl.ds(pl.multiple_of(j * t, t), t), :]
        return lax.dot_general(q2, kb, (((1,), (1,)), ((), ())), preferred_element_type=F32)

    def lane_max(s, m):
        for c in range(t // LANES):
            m = jnp.maximum(m, s[:, c * LANES:(c + 1) * LANES])
        return m

    def accumulate(j, s, mrow):
        p = jnp.exp2(s - mrow).astype(BF16)
        v1 = jnp.concatenate([v_ref[pl.ds(pl.multiple_of(j * t, t), t), :], ones], axis=1)
        acc_ref[...] += jnp.dot(p, v1, preferred_element_type=F32)

    def key_cols(j):
        return pl.ds(pl.multiple_of(j * t, t), t)

    def first_pass(j, m):
        s = scores(j)
        s_ref[:, key_cols(j)] = s
        return lane_max(s, m)

    s_diag = jnp.where(on_or_below, scores(i), NEG_BIG)
    m = lax.fori_loop(0, i, first_pass, lane_max(s_diag, jnp.full((2 * t, LANES), NEG_BIG, F32)))
    mrow = jnp.max(m, axis=-1, keepdims=True)
    acc_ref[...] = jnp.zeros_like(acc_ref)
    accumulate(i, s_diag, mrow)

    def second_pass(j, carry):
        accumulate(j, s_ref[:, key_cols(j)], mrow)
        return carry

    lax.fori_loop(0, i, second_pass, 0)
    acc = acc_ref[...]
    o_all = acc[:, :dv] / acc[:, dv:]
    lam = _lambda(lq1_ref, lk1_ref, lq2_ref, lk2_ref, lam_init)
    o = o_all[:t] - lam * o_all[t:]
    o_ref[...] = _sub_norm(o, subg_ref[...], lam_init).astype(BF16)


def _attn_prompt(dq, dkb, dvb, lam_vecs, subg, batch, seq, dh, lam_init):
    t = ATTN_TILE if seq % ATTN_TILE == 0 else seq
    nq = seq // t
    width = 2 * dh
    vec = [_resident(v.shape) for v in lam_vecs]
    return pl.pallas_call(
        functools.partial(_attn_prompt_body, dh=dh, lam_init=lam_init),
        grid=(batch, H_DIFF, nq),
        in_specs=vec + [_resident(subg.shape),
                        pl.BlockSpec((t, width), lambda b, h, i: (b * nq + i, h)),
                        pl.BlockSpec((seq, width), lambda b, h, i: (b, h)),
                        pl.BlockSpec((seq, width), lambda b, h, i: (b, h))],
        out_specs=pl.BlockSpec((t, width), lambda b, h, i: (b * nq + i, h)),
        out_shape=jax.ShapeDtypeStruct((batch * seq, H_DIFF * width), BF16),
        scratch_shapes=[pltpu.VMEM((2 * t, 2 * width), F32), pltpu.VMEM((2 * t, seq), F32)],
        compiler_params=_params(("parallel", "parallel", "arbitrary")),
        name="diff_attn_prompt",
    )(*lam_vecs, subg, dq, dkb, dvb)


class _DecodeSide(NamedTuple):
    first_sample: int
    n_samples: int
    lam_init: float


def _decode_softmax(probs, q8, knew, n_cols):
    piece = DEC_SOFTMAX_PIECE if n_cols % DEC_SOFTMAX_PIECE == 0 else n_cols
    lane = lax.broadcasted_iota(jnp.int32, (2 * H_DIFF, piece), 1)
    row = lax.broadcasted_iota(jnp.int32, (2 * H_DIFF, piece), 0)
    own_head = (lane & (H_DIFF - 1)) == (row & (H_DIFF - 1))
    s_new = jnp.sum(q8.astype(F32) * knew, axis=-1, keepdims=True)
    mx = s_new
    for c in range(n_cols // piece):
        s = jnp.where(own_head, probs[:, c * piece:(c + 1) * piece], NEG_BIG)
        mx = jnp.maximum(mx, jnp.max(s, axis=-1, keepdims=True))
    e_new = jnp.exp2(s_new - mx)
    total = e_new
    for c in range(n_cols // piece):
        e = jnp.exp2(jnp.where(own_head, probs[:, c * piece:(c + 1) * piece], NEG_BIG) - mx)
        probs[:, c * piece:(c + 1) * piece] = e
        total = total + jnp.sum(e, axis=-1, keepdims=True)
    return e_new, 1.0 / total


def _mix_ffn_body(*refs, widths, projected, side):
    if side is not None:
        pt_ref, refs = refs[0], refs[1:]
    n = len(widths)
    x_ref = refs[0]
    a_refs = refs[1:1 + n]
    if projected:
        wo_ref = None
        g_ref, wup_ref, wdn_ref = refs[1 + n:4 + n]
        rest = refs[4 + n:]
    else:
        wo_ref, g_ref, wup_ref, wdn_ref = refs[1 + n:5 + n]
        rest = refs[5 + n:]
    d_ff = wup_ref.shape[1]
    fc = FF_CHUNK if d_ff % FF_CHUNK == 0 else d_ff
    n_ff = d_ff // fc

    def ffn(after_chunk):
        a = a_refs[0][...] if n == 1 else jnp.concatenate([r[...] for r in a_refs], axis=-1)
        y = x_ref[...] + (a if projected else jnp.dot(a, wo_ref[...], preferred_element_type=F32))
        hb = _rms(y, g_ref[...]).astype(BF16)
        for c in range(n_ff):
            hid = jnp.dot(hb, wup_ref[:, c * fc:(c + 1) * fc], preferred_element_type=F32)
            act = jnp.square(jnp.maximum(hid, 0.0)).astype(BF16)
            y = y + jnp.dot(act, wdn_ref[c * fc:(c + 1) * fc, :], preferred_element_type=F32)
            after_chunk(c)
        y_ref[...] = y

    if side is None:
        (y_ref,) = rest
        ffn(lambda c: None)
        return

    (lq1_ref, lk1_ref, lq2_ref, lk2_ref, subg_ref, q8_ref, knew_ref, vnew_ref, ck_ref, cv_ref,
     y_ref, do_ref, ring, sems, probs, acc) = rest
    step = pl.program_id(0)
    n_steps = pl.num_programs(0)
    slots, cp, page_rows, _ = ring.shape
    n_pages = pt_ref.shape[1]
    n_chunks = n_pages // cp
    per_ff = n_chunks // n_ff
    look = slots
    b = step // 2

    def page_copy(phase, at_step, chunk, i):
        sample = side.first_sample + at_step // 2
        src = ck_ref if phase == 0 else cv_ref
        slot = chunk % slots
        return pltpu.make_async_copy(src.at[pt_ref[sample, chunk * cp + i]], ring.at[slot, i], sems.at[slot])

    def start_chunk(phase, at_step, chunk):
        for i in range(cp):
            page_copy(phase, at_step, chunk, i).start(priority=1)

    def use_chunk(phase, chunk):
        cols = slice(chunk * cp * page_rows, (chunk + 1) * cp * page_rows)
        pages = ring[chunk % slots].reshape(cp * page_rows, ring.shape[-1]).astype(BF16)
        if phase == 0:
            probs[:, cols] = lax.dot_general(q8_ref[b], pages, (((1,), (1,)), ((), ())), preferred_element_type=F32)
        else:
            acc[...] += jnp.dot(probs[:, cols].astype(BF16), pages, preferred_element_type=F32)

    def after_chunk(phase, c):
        chunks = range(c * per_ff, (c + 1) * per_ff)
        for chunk in chunks:
            for i in range(cp):
                page_copy(phase, step, chunk, i).wait()
        for chunk in chunks:
            use_chunk(phase, chunk)
        for chunk in chunks:
            ahead = chunk + look
            if ahead < n_chunks:
                start_chunk(phase, step, ahead)
            else:
                @pl.when(step + 1 < n_steps)
                def _():
                    start_chunk(1 - phase, step + 1, ahead - n_chunks)

    @pl.when(step == 0)
    def _():
        for chunk in range(look):
            start_chunk(0, step, chunk)

    @pl.when(step % 2 == 0)
    def _():
        ffn(functools.partial(after_chunk, 0))

    @pl.when(step % 2 == 1)
    def _():
        e_new, inv = _decode_softmax(probs, q8_ref[b], knew_ref[b], n_pages * page_rows)
        acc[...] = jnp.zeros_like(acc)
        ffn(functools.partial(after_chunk, 1))
        vnew = vnew_ref[b]
        a = (acc[...] + e_new * jnp.concatenate([vnew, vnew], axis=0)) * inv
        lam = _lambda(lq1_ref, lk1_ref, lq2_ref, lk2_ref, side.lam_init)
        o = a[:H_DIFF] - lam * a[H_DIFF:]
        do_ref[b] = _sub_norm(o, subg_ref[...], side.lam_init)


def _mix_ffn(x, parts, w_o, g, w_up, w_dn, tm, side=None, decode_ops=None):
    rows, d = x.shape
    widths = tuple(p.shape[1] for p in parts)
    n_steps = rows // tm
    row = lambda width: pl.BlockSpec((tm, width), lambda i, *_: (i, 0))
    weights = ([] if w_o is None else [w_o]) + [g, w_up, w_dn]
    in_specs = [row(d)] + [row(w) for w in widths] + [_resident(w.shape) for w in weights]
    body = functools.partial(_mix_ffn_body, widths=widths, projected=w_o is None, side=side)
    if side is None:
        return pl.pallas_call(
            body, grid=(n_steps,), in_specs=in_specs, out_specs=row(d),
            out_shape=jax.ShapeDtypeStruct((rows, d), F32),
            compiler_params=_params(("parallel",)), name="mix_ffn",
        )(x, *parts, *weights)

    page_table, cache_k, cache_v, q8, knew8, vnew, lam_vecs, subg = decode_ops
    n_pages = page_table.shape[1]
    _, page_rows, width = cache_k.shape
    assert n_steps == 2 * side.n_samples and n_pages % DEC_CHUNK_PAGES == 0
    assert (n_pages // DEC_CHUNK_PAGES) % (w_up.shape[1] // FF_CHUNK) == 0
    assert (n_pages // DEC_CHUNK_PAGES) % DEC_SLOTS == 0
    lo, hi = side.first_sample, side.first_sample + side.n_samples
    local = [q8[lo:hi], knew8[lo:hi], vnew[lo:hi]]
    any_spec = pl.BlockSpec(memory_space=pl.ANY)
    grid_spec = pltpu.PrefetchScalarGridSpec(
        num_scalar_prefetch=1,
        grid=(n_steps,),
        in_specs=in_specs + [_resident(v.shape) for v in lam_vecs] + [_resident(subg.shape)]
        + [_resident(a.shape) for a in local] + [any_spec, any_spec],
        out_specs=[row(d), pl.BlockSpec((side.n_samples, H_DIFF, width), lambda *_: (0, 0, 0))],
        scratch_shapes=[pltpu.VMEM((DEC_SLOTS, DEC_CHUNK_PAGES, page_rows, width), F32),
                        pltpu.SemaphoreType.DMA((DEC_SLOTS,)),
                        pltpu.VMEM((2 * H_DIFF, n_pages * page_rows), F32),
                        pltpu.VMEM((2 * H_DIFF, width), F32)],
    )
    return pl.pallas_call(
        body, grid_spec=grid_spec,
        out_shape=[jax.ShapeDtypeStruct((rows, d), F32), jax.ShapeDtypeStruct((side.n_samples, H_DIFF, width), F32)],
        compiler_params=_params(("arbitrary",)), name="mix_ffn_decode",
    )(page_table, x, *parts, *weights, *lam_vecs, subg, *local, cache_k, cache_v)


def _gelu(x):
    return 0.5 * x * (1.0 + lax.erf(x * (2.0 ** -0.5)))


def _gmlp_body(x_ref, g_ref, w_ref, lng_ref, lnb_ref, ws_ref, bs_ref, wout_ref, mix_ref, *rest, e, single_token):
    if single_token:
        vn_ref, vbuf, a_ref = rest
    else:
        vbuf, a_ref = rest
    hb = _rms(x_ref[...], g_ref[...]).astype(BF16)
    rows = hb.shape[0]
    gc = e // GMLP_GROUPS
    pc = 2 * gc
    for pair in range(GMLP_GROUPS // 2):
        vbuf[:, pair * pc:(pair + 1) * pc] = _gelu(
            jnp.dot(hb, w_ref[:, e + pair * pc:e + (pair + 1) * pc], preferred_element_type=F32))
    rb = min(rows, GMLP_CHUNK)
    for r in range(rows // rb):
        rsl = slice(r * rb, (r + 1) * rb)
        v = vbuf[rsl, :]
        vc = v - jnp.mean(v, axis=-1, keepdims=True)
        vn = vc * lax.rsqrt(jnp.mean(vc * vc, axis=-1, keepdims=True) + EPS) * lng_ref[...] + lnb_ref[...]
        vbuf[rsl, :] = vn
        if single_token:
            vn_ref[rsl, :] = vn
    if not single_token:
        t = lax.broadcasted_iota(jnp.int32, (GMLP_CHUNK, GMLP_CHUNK), 0)
        s_ = lax.broadcasted_iota(jnp.int32, (GMLP_CHUNK, GMLP_CHUNK), 1)
        tril = t >= s_
    for grp in range(GMLP_GROUPS):
        cols = slice(grp * gc, (grp + 1) * gc)
        if grp % 2 == 0:
            u2 = _gelu(jnp.dot(hb, w_ref[:, grp * gc:grp * gc + pc], preferred_element_type=F32))
        u = u2[:, (grp % 2) * gc:(grp % 2 + 1) * gc]
        if single_token:
            a_ref[:, cols] = (u * (vbuf[:, cols] * ws_ref[grp][0:1, 0:1] + bs_ref[0:1, grp:grp + 1])).astype(BF16)
        else:
            wsg = jnp.where(tril, ws_ref[grp], 0.0).astype(BF16)
            bias = bs_ref[:, grp:grp + 1]
            for r in range(rows // GMLP_CHUNK):
                rsl = slice(r * GMLP_CHUNK, (r + 1) * GMLP_CHUNK)
                s = jnp.dot(wsg, vbuf[rsl, cols].astype(BF16), preferred_element_type=F32) + bias
                a_ref[rsl, cols] = (u[rsl] * s).astype(BF16)
    mix_ref[...] = jnp.dot(a_ref[...], wout_ref[...], preferred_element_type=F32)


def _gmlp(x, g, w, lng, lnb, ws, bs_t, w_out, tm, single_token):
    rows, d = x.shape
    e = w.shape[1] // 2
    row = lambda width: pl.BlockSpec((tm, width), lambda i: (i, 0))
    out_specs = [row(d)]
    out_shape = [jax.ShapeDtypeStruct((rows, d), F32)]
    if single_token:
        out_specs.append(row(e))
        out_shape.append(jax.ShapeDtypeStruct((rows, e), F32))
    return pl.pallas_call(
        functools.partial(_gmlp_body, e=e, single_token=single_token),
        grid=(rows // tm,),
        in_specs=[row(d), _resident(g.shape), _resident(w.shape), _resident(lng.shape), _resident(lnb.shape),
                  _resident(ws.shape), _resident(bs_t.shape), _resident(w_out.shape)],
        out_specs=out_specs,
        out_shape=out_shape,
        scratch_shapes=[pltpu.VMEM((tm, e), F32), pltpu.VMEM((tm, e), BF16)],
        compiler_params=_params(("parallel",)),
        name="gmlp_sample" if single_token else "gmlp_prompt",
    )(x, g, w, lng, lnb, ws, bs_t, w_out)


def _rope_tables(pos, d):
    inv = 1.0 / (ROPE_THETA ** (jnp.arange(0, d, 2, dtype=F32) / d))
    ang = pos.astype(F32)[:, None] * inv[None, :]
    c, s = jnp.cos(ang), jnp.sin(ang)
    reps = LANES // d
    return jnp.tile(jnp.concatenate([c, c], -1), (1, reps)), jnp.tile(jnp.concatenate([-s, s], -1), (1, reps))


def kernel(x_prompt, x_sample, state_ret, cache_k, cache_v, page_table, rms_mix_g, rms_ffn_g, w_ffn_up, w_ffn_down,
           w_in_even, w_out_even, q_norm_g, k_norm_g, lam_q1, lam_k1, lam_q2, lam_k2, subln_g,
           w_in_odd, ln_v_g, ln_v_b, w_s, b_s, w_out_odd):
    bp, sp, d = x_prompt.shape
    bs, ls, _ = x_sample.shape
    depth = rms_mix_g.shape[0]
    dk_ret, dv_ret = state_ret.shape[-2:]
    n_pool, page_size = cache_k.shape[1:3]
    dh = cache_k.shape[-1] // 2
    dv_diff = cache_v.shape[-1]
    assert dk_ret == LANES and 2 * dh == LANES and dv_diff == LANES and ls == 1
    ret_qk, ret_v = H_RET * dk_ret, H_RET * dv_ret
    diff_qk, diff_v = H_DIFF * 2 * dh, H_DIFF * dv_diff
    dims = (ret_qk, ret_v, diff_qk, diff_v, dh)
    past = page_table.shape[1] * page_size
    tm = ROW_TILE if (bp * sp) % ROW_TILE == 0 and sp % ROW_TILE == 0 else sp
    rows_s = bs * ls

    pos_p = jnp.arange(sp, dtype=jnp.int32)
    pos_s = jnp.tile(past + jnp.arange(ls, dtype=jnp.int32), bs)
    tabs_p = _rope_tables(pos_p, dk_ret) + _rope_tables(pos_p, dh)
    tabs_s = _rope_tables(pos_s, dk_ret) + _rope_tables(pos_s, dh)
    gidx = jnp.arange(2 * LANES) // dh
    grp = (gidx[:, None] == gidx[None, :]).astype(BF16)

    assert depth == 2, "the decode attention of layer 0 is spread over the prompt MLP calls of layers 0 and 1"
    yp = x_prompt.reshape(bp * sp, d)
    ys = x_sample.reshape(rows_s, d)
    ffn_w = [(rms_ffn_g[l][None], w_ffn_up[l].astype(BF16), w_ffn_down[l].astype(BF16)) for l in range(depth)]

    lam_init = 0.8 - 0.6 * math.exp(-0.3 * 0)
    g_mix = rms_mix_g[0][None]
    w_in = w_in_even[0].astype(BF16)
    w_out = w_out_even[0].astype(BF16)
    qg = jnp.tile(q_norm_g[0], LANES // dh)[None]
    kg = jnp.tile(k_norm_g[0], LANES // dh)[None]
    lam_vecs = [lam_q1[0][None], lam_k1[0][None], lam_q2[0][None], lam_k2[0][None]]
    subg = subln_g[0][None]

    rq, rk, rv, gate, dq, dk_p, dv_p, dkb, dvb = _in_proj(yp, g_mix, w_in, tabs_p, sp // tm, qg, kg, grp, tm, dims)
    ro_p, s_fin = _retention_prompt(rq, rk, rv, gate, bp, sp, dk_ret, dv_ret)
    do_p = _attn_prompt(dq, dkb, dvb, lam_vecs, subg, bp, sp, dh, lam_init)

    rq, rk, rv, gate, dq, dk_s, dv_s, _, _ = _in_proj(ys, g_mix, w_in, tabs_s, 1, qg, kg, grp, rows_s, dims)
    s_new, ro_s = _retention_step(
        state_ret[0],
        rq.astype(F32).reshape(bs, H_RET, dk_ret, 1), rk.astype(F32).reshape(bs, H_RET, dk_ret, 1),
        rv.reshape(bs, H_RET, 1, dv_ret), gate.reshape(bs, H_RET, 1, dv_ret), 4 if bs % 4 == 0 else 1)
    q4 = dq.reshape(bs, H_DIFF, 2 * dh)
    half = (jnp.arange(2 * dh) < dh)[None, None, :]
    q8 = jnp.concatenate([jnp.where(half, q4, 0), jnp.where(half, 0, q4)], axis=1)
    k4 = dk_s
    decode_ops = (page_table,
                  cache_k[0].reshape(n_pool, page_size * H_DIFF, 2 * dh),
                  cache_v[0].reshape(n_pool, page_size * H_DIFF, dv_diff),
                  q8, jnp.concatenate([k4, k4], axis=1), dv_s.reshape(bs, H_DIFF, dv_diff), lam_vecs, subg)
    n_first = bs // 2

    yp, do_a = _mix_ffn(yp, [ro_p, do_p], w_out, *ffn_w[0], tm, _DecodeSide(0, n_first, lam_init), decode_ops)

    g_mix1 = rms_mix_g[1][None]
    w_in1 = w_in_odd[0].astype(BF16)
    w_out1 = w_out_odd[0].astype(BF16)
    lng, lnb = ln_v_g[0][None], ln_v_b[0][None]
    bs_t = b_s[0].T
    (mix,) = _gmlp(yp, g_mix1, w_in1, lng, lnb, w_s[0], bs_t, w_out1, min(tm, GMLP_TILE), False)
    yp, do_b = _mix_ffn(yp, [mix], None, *ffn_w[1], tm, _DecodeSide(n_first, bs - n_first, lam_init), decode_ops)

    do_s = jnp.concatenate([do_a, do_b], axis=0).astype(BF16)
    ys = _mix_ffn(ys, [ro_s.reshape(rows_s, ret_v), do_s.reshape(rows_s, diff_v)], w_out, *ffn_w[0], rows_s)
    mix, gv = _gmlp(ys, g_mix1, w_in1, lng, lnb, w_s[0], bs_t, w_out1, rows_s, True)
    ys = _mix_ffn(ys, [mix], None, *ffn_w[1], rows_s)

    return (yp.reshape(bp, sp, d), ys.reshape(bs, ls, d), s_fin[None], s_new[None],
            dk_p.reshape(1, bp, sp, H_DIFF, 2 * dh), dv_p.reshape(1, bp, sp, H_DIFF, dv_diff),
            dk_s.reshape(1, bs, ls, H_DIFF, 2 * dh), dv_s.reshape(1, bs, ls, H_DIFF, dv_diff),
            gv.reshape(1, bs, ls, -1))
```

```python
import functools
import math
from typing import NamedTuple

import jax
import jax.numpy as jnp
from jax import lax
from jax.experimental import pallas as pl
from jax.experimental.pallas import tpu as pltpu

F32 = jnp.float32
BF16 = jnp.bfloat16
EPS = 1e-6
ROPE_THETA = 10000.0
H_RET = 4
H_DIFF = 4
GMLP_GROUPS = 8
GMLP_CHUNK = 128
LANES = 128
VMEM_LIMIT_BYTES = 58 * 1024 * 1024
NEG_BIG = -1e30

ROW_TILE = 512
GMLP_TILE = 512
RET_CHUNK = 256
ATTN_TILE = 512
FF_CHUNK = 512
DEC_CHUNK_PAGES = 8
DEC_SLOTS = 4
DEC_SOFTMAX_PIECE = 8192


def _params(semantics):
    return pltpu.CompilerParams(dimension_semantics=semantics, vmem_limit_bytes=VMEM_LIMIT_BYTES)


def _resident(shape):
    return pl.BlockSpec(shape, lambda *_: (0,) * len(shape), pipeline_mode=pl.Buffered(1))


def _rms(x, g):
    return x * lax.rsqrt(jnp.mean(x * x, axis=-1, keepdims=True) + EPS) * g


def _log_gamma(h):
    return math.log1p(-(2.0 ** (-5.0 - h)))


def _in_proj_body(x_ref, g_ref, w_ref, cr_ref, sr_ref, cd_ref, sd_ref, qg_ref, kg_ref, grp_ref,
                  rq_ref, rk_ref, rv_ref, gate_ref, dq_ref, dk_ref, dv_ref, dkb_ref, dvb_ref,
                  *, ret_qk, ret_v, diff_qk, diff_v, dh):
    hb = _rms(x_ref[...], g_ref[...]).astype(BF16)
    rows = hb.shape[0]

    def proj(lo, width):
        return jnp.dot(hb, w_ref[:, lo:lo + width].astype(BF16), preferred_element_type=F32)

    cr, sr = cr_ref[...], sr_ref[...]
    for col0, out_ref in ((0, rq_ref), (ret_qk, rk_ref)):
        p = proj(col0, ret_qk)
        for c in range(ret_qk // LANES):
            s = p[:, c * LANES:(c + 1) * LANES]
            out_ref[:, c * LANES:(c + 1) * LANES] = (s * cr + pltpu.roll(s, LANES // 2, 1) * sr).astype(BF16)
    rv_ref[...] = proj(2 * ret_qk, ret_v).astype(BF16)
    rg = proj(2 * ret_qk + ret_v, ret_v)
    gate_ref[...] = (rg * jax.nn.sigmoid(rg)).astype(BF16)

    cd, sd = cd_ref[...], sd_ref[...]
    lane = lax.broadcasted_iota(jnp.int32, (rows, LANES), 1)
    first_half = (lane & (dh // 2)) == 0
    grp = grp_ref[...]
    gw = grp.shape[0]
    base = 2 * ret_qk + 2 * ret_v

    def norm_rope(col0, gain):
        p = proj(col0, diff_qk)
        slabs = []
        for c in range(diff_qk // gw):
            s = p[:, c * gw:(c + 1) * gw]
            ssq = jnp.dot((s * s).astype(BF16), grp, preferred_element_type=F32)
            n = s * lax.rsqrt(ssq * (1.0 / dh) + EPS)
            for d in range(gw // LANES):
                t = n[:, d * LANES:(d + 1) * LANES] * gain
                partner = jnp.where(first_half, pltpu.roll(t, LANES - dh // 2, 1), pltpu.roll(t, dh // 2, 1))
                slabs.append(t * cd + partner * sd)
        return slabs

    q_scale = dh ** -0.5 * math.log2(math.e)
    for c, s in enumerate(norm_rope(base, qg_ref[...])):
        dq_ref[:, c * LANES:(c + 1) * LANES] = (s * q_scale).astype(BF16)
    for c, s in enumerate(norm_rope(base + diff_qk, kg_ref[...])):
        dk_ref[:, c, :] = s
        dkb_ref[:, c * LANES:(c + 1) * LANES] = s.astype(BF16)
    dv = proj(base + 2 * diff_qk, diff_v)
    for c in range(diff_v // LANES):
        dv_ref[:, c, :] = dv[:, c * LANES:(c + 1) * LANES]
    dvb_ref[...] = dv.astype(BF16)


def _in_proj(x, g, w, tabs, tab_blocks, qg, kg, grp, tm, dims):
    rows, d = x.shape
    ret_qk, ret_v, diff_qk, diff_v, dh = dims
    n = rows // tm
    row = lambda width: pl.BlockSpec((tm, width), lambda i: (i, 0))
    tab = pl.BlockSpec((tm, LANES), lambda i: (i % tab_blocks, 0))
    body = functools.partial(_in_proj_body, ret_qk=ret_qk, ret_v=ret_v, diff_qk=diff_qk, diff_v=diff_v, dh=dh)
    shapes = [((rows, ret_qk), BF16), ((rows, ret_qk), BF16), ((rows, ret_v), BF16), ((rows, ret_v), BF16),
              ((rows, diff_qk), BF16), ((rows, diff_qk // LANES, LANES), F32), ((rows, diff_v // LANES, LANES), F32),
              ((rows, diff_qk), BF16), ((rows, diff_v), BF16)]
    out_spec = lambda s: row(s[1]) if len(s) == 2 else pl.BlockSpec((tm,) + s[1:], lambda i: (i, 0, 0))
    return pl.pallas_call(
        body,
        grid=(n,),
        in_specs=[row(d), _resident(g.shape), _resident(w.shape), tab, tab, tab, tab,
                  _resident(qg.shape), _resident(kg.shape), _resident(grp.shape)],
        out_specs=[out_spec(s) for s, _ in shapes],
        out_shape=[jax.ShapeDtypeStruct(s, t) for s, t in shapes],
        compiler_params=_params(("parallel",)),
        name="in_proj",
    )(x, g, w, *tabs, qg, kg, grp)


def _retention_body(q_ref, k_ref, v_ref, gate_ref, ro_ref, sfin_ref, state, dmask, *, dk, dv):
    b, c = pl.program_id(0), pl.program_id(1)
    chunk = q_ref.shape[0]
    scale = dk ** -0.5

    @pl.when((b == 0) & (c == 0))
    def _():
        i = lax.broadcasted_iota(jnp.int32, (chunk, chunk), 0)
        j = lax.broadcasted_iota(jnp.int32, (chunk, chunk), 1)
        causal = i >= j
        diff = jnp.where(causal, i - j, 0).astype(F32)
        for h in range(H_RET):
            dmask[h] = jnp.where(causal, jnp.exp(diff * _log_gamma(h)), 0.0) * scale

    @pl.when(c == 0)
    def _():
        state[...] = jnp.zeros_like(state)

    idx = lax.broadcasted_iota(jnp.int32, (chunk, 1), 0).astype(F32)
    for h in range(H_RET):
        lg = _log_gamma(h)
        q = q_ref[:, h * dk:(h + 1) * dk]
        k = k_ref[:, h * dk:(h + 1) * dk]
        v = v_ref[:, h * dv:(h + 1) * dv]
        s_prev = state[h]
        scores = lax.dot_general(q, k, (((1,), (1,)), ((), ())), preferred_element_type=F32) * dmask[h]
        inner = jnp.dot(scores.astype(BF16), v, preferred_element_type=F32)
        cross = jnp.dot(q, s_prev.astype(BF16), preferred_element_type=F32) * jnp.exp((idx + 1.0) * lg)
        o = inner + cross
        zeta = jnp.exp((chunk - 1.0 - idx) * lg) * scale
        kz_t = (k.astype(F32) * zeta).T.astype(BF16)
        state[h] = math.exp(chunk * lg) * s_prev + jnp.dot(kz_t, v, preferred_element_type=F32)
        oc = o - jnp.mean(o, axis=-1, keepdims=True)
        on = oc * lax.rsqrt(jnp.mean(oc * oc, axis=-1, keepdims=True) + EPS)
        ro_ref[:, h * dv:(h + 1) * dv] = (on * gate_ref[:, h * dv:(h + 1) * dv].astype(F32)).astype(BF16)

    @pl.when(c == pl.num_programs(1) - 1)
    def _():
        sfin_ref[0] = state[...]


def _retention_prompt(rq, rk, rv, gate, batch, seq, dk, dv):
    chunk = RET_CHUNK if seq % RET_CHUNK == 0 else seq
    nc = seq // chunk
    blk = lambda width: pl.BlockSpec((chunk, width), lambda b, c: (b * nc + c, 0))
    return pl.pallas_call(
        functools.partial(_retention_body, dk=dk, dv=dv),
        grid=(batch, nc),
        in_specs=[blk(H_RET * dk), blk(H_RET * dk), blk(H_RET * dv), blk(H_RET * dv)],
        out_specs=[blk(H_RET * dv), pl.BlockSpec((1, H_RET, dk, dv), lambda b, c: (b, 0, 0, 0))],
        out_shape=[jax.ShapeDtypeStruct((batch * seq, H_RET * dv), BF16),
                   jax.ShapeDtypeStruct((batch, H_RET, dk, dv), F32)],
        scratch_shapes=[pltpu.VMEM((H_RET, dk, dv), F32), pltpu.VMEM((H_RET, chunk, chunk), F32)],
        compiler_params=_params(("arbitrary", "arbitrary")),
        name="retention_prompt",
    )(rq, rk, rv, gate)


def _retention_step_body(s_ref, q_ref, k_ref, v_ref, gate_ref, snew_ref, ro_ref, *, dk):
    scale = dk ** -0.5
    for h in range(H_RET):
        s_new = math.exp(_log_gamma(h)) * s_ref[:, h] + (k_ref[:, h] * scale) * v_ref[:, h]
        snew_ref[:, h] = s_new
        o = jnp.sum(q_ref[:, h] * s_new, axis=1, keepdims=True)
        oc = o - jnp.mean(o, axis=-1, keepdims=True)
        on = oc * lax.rsqrt(jnp.mean(oc * oc, axis=-1, keepdims=True) + EPS)
        ro_ref[:, h] = (on * gate_ref[:, h].astype(F32)).astype(BF16)


def _retention_step(state, q_col, k_col, v_row, gate_row, bt):
    nb, _, dk, dv = state.shape
    col = pl.BlockSpec((bt, H_RET, dk, 1), lambda i: (i, 0, 0, 0))
    rowb = pl.BlockSpec((bt, H_RET, 1, dv), lambda i: (i, 0, 0, 0))
    st = pl.BlockSpec((bt, H_RET, dk, dv), lambda i: (i, 0, 0, 0))
    return pl.pallas_call(
        functools.partial(_retention_step_body, dk=dk),
        grid=(nb // bt,),
        in_specs=[st, col, col, rowb, rowb],
        out_specs=[st, rowb],
        out_shape=[jax.ShapeDtypeStruct(state.shape, F32), jax.ShapeDtypeStruct((nb, H_RET, 1, dv), BF16)],
        compiler_params=_params(("parallel",)),
        name="retention_step",
    )(state, q_col, k_col, v_row, gate_row)


def _lambda(lq1_ref, lk1_ref, lq2_ref, lk2_ref, lam_init):
    a = jnp.sum(lq1_ref[...] * lk1_ref[...], axis=-1, keepdims=True)
    b = jnp.sum(lq2_ref[...] * lk2_ref[...], axis=-1, keepdims=True)
    return jnp.exp(a) - jnp.exp(b) + lam_init


def _sub_norm(o, subg, lam_init):
    return o * lax.rsqrt(jnp.mean(o * o, axis=-1, keepdims=True) + EPS) * subg * (1.0 - lam_init)


def _attn_prompt_body(lq1_ref, lk1_ref, lq2_ref, lk2_ref, subg_ref, q_ref, k_ref, v_ref, o_ref, acc_ref, s_ref,
                      *, dh, lam_init):
    i = pl.program_id(2)
    t = q_ref.shape[0]
    dv = v_ref.shape[1]
    q = q_ref[...]
    lane = lax.broadcasted_iota(jnp.int32, q.shape, 1)
    zero = jnp.zeros_like(q)
    q2 = jnp.concatenate([jnp.where(lane < dh, q, zero), jnp.where(lane >= dh, q, zero)], axis=0)
    row = lax.broadcasted_iota(jnp.int32, (2 * t, t), 0)
    col = lax.broadcasted_iota(jnp.int32, (2 * t, t), 1)
    on_or_below = col <= jnp.where(row >= t, row - t, row)
    ones = jnp.ones((t, LANES), BF16)

    def scores(j):
        kb = k_ref[pl.ds(pl.multiple_of(j * t, t), t), :]
        return lax.dot_general(q2, kb, (((1,), (1,)), ((), ())), preferred_element_type=F32)

    def lane_max(s, m):
        for c in range(t // LANES):
            m = jnp.maximum(m, s[:, c * LANES:(c + 1) * LANES])
        return m

    def accumulate(j, s, mrow):
        p = jnp.exp2(s - mrow).astype(BF16)
        v1 = jnp.concatenate([v_ref[pl.ds(pl.multiple_of(j * t, t), t), :], ones], axis=1)
        acc_ref[...] += jnp.dot(p, v1, preferred_element_type=F32)

    def key_cols(j):
        return pl.ds(pl.multiple_of(j * t, t), t)

    def first_pass(j, m):
        s = scores(j)
        s_ref[:, key_cols(j)] = s
        return lane_max(s, m)

    s_diag = jnp.where(on_or_below, scores(i), NEG_BIG)
    m = lax.fori_loop(0, i, first_pass, lane_max(s_diag, jnp.full((2 * t, LANES), NEG_BIG, F32)))
    mrow = jnp.max(m, axis=-1, keepdims=True)
    acc_ref[...] = jnp.zeros_like(acc_ref)
    accumulate(i, s_diag, mrow)

    def second_pass(j, carry):
        accumulate(j, s_ref[:, key_cols(j)], mrow)
        return carry

    lax.fori_loop(0, i, second_pass, 0)
    acc = acc_ref[...]
    o_all = acc[:, :dv] / acc[:, dv:]
    lam = _lambda(lq1_ref, lk1_ref, lq2_ref, lk2_ref, lam_init)
    o = o_all[:t] - lam * o_all[t:]
    o_ref[...] = _sub_norm(o, subg_ref[...], lam_init).astype(BF16)


def _attn_prompt(dq, dkb, dvb, lam_vecs, subg, batch, seq, dh, lam_init):
    t = ATTN_TILE if seq % ATTN_TILE == 0 else seq
    nq = seq // t
    width = 2 * dh
    vec = [_resident(v.shape) for v in lam_vecs]
    return pl.pallas_call(
        functools.partial(_attn_prompt_body, dh=dh, lam_init=lam_init),
        grid=(batch, H_DIFF, nq),
        in_specs=vec + [_resident(subg.shape),
                        pl.BlockSpec((t, width), lambda b, h, i: (b * nq + i, h)),
                        pl.BlockSpec((seq, width), lambda b, h, i: (b, h)),
                        pl.BlockSpec((seq, width), lambda b, h, i: (b, h))],
        out_specs=pl.BlockSpec((t, width), lambda b, h, i: (b * nq + i, h)),
        out_shape=jax.ShapeDtypeStruct((batch * seq, H_DIFF * width), BF16),
        scratch_shapes=[pltpu.VMEM((2 * t, 2 * width), F32), pltpu.VMEM((2 * t, seq), F32)],
        compiler_params=_params(("parallel", "parallel", "arbitrary")),
        name="diff_attn_prompt",
    )(*lam_vecs, subg, dq, dkb, dvb)


class _DecodeSide(NamedTuple):
    first_sample: int
    n_samples: int
    lam_init: float


def _decode_softmax(probs, q8, knew, n_cols):
    piece = DEC_SOFTMAX_PIECE if n_cols % DEC_SOFTMAX_PIECE == 0 else n_cols
    lane = lax.broadcasted_iota(jnp.int32, (2 * H_DIFF, piece), 1)
    row = lax.broadcasted_iota(jnp.int32, (2 * H_DIFF, piece), 0)
    own_head = (lane & (H_DIFF - 1)) == (row & (H_DIFF - 1))
    s_new = jnp.sum(q8.astype(F32) * knew, axis=-1, keepdims=True)
    mx = s_new
    for c in range(n_cols // piece):
        s = jnp.where(own_head, probs[:, c * piece:(c + 1) * piece], NEG_BIG)
        mx = jnp.maximum(mx, jnp.max(s, axis=-1, keepdims=True))
    e_new = jnp.exp2(s_new - mx)
    total = e_new
    for c in range(n_cols // piece):
        e = jnp.exp2(jnp.where(own_head, probs[:, c * piece:(c + 1) * piece], NEG_BIG) - mx)
        probs[:, c * piece:(c + 1) * piece] = e
        total = total + jnp.sum(e, axis=-1, keepdims=True)
    return e_new, 1.0 / total


def _mix_ffn_body(*refs, widths, projected, side):
    if side is not None:
        pt_ref, refs = refs[0], refs[1:]
    n = len(widths)
    x_ref = refs[0]
    a_refs = refs[1:1 + n]
    if projected:
        wo_ref = None
        g_ref, wup_ref, wdn_ref = refs[1 + n:4 + n]
        rest = refs[4 + n:]
    else:
        wo_ref, g_ref, wup_ref, wdn_ref = refs[1 + n:5 + n]
        rest = refs[5 + n:]
    d_ff = wup_ref.shape[1]
    fc = FF_CHUNK if d_ff % FF_CHUNK == 0 else d_ff
    n_ff = d_ff // fc

    def ffn(after_chunk):
        a = a_refs[0][...] if n == 1 else jnp.concatenate([r[...] for r in a_refs], axis=-1)
        y = x_ref[...] + (a if projected else jnp.dot(a, wo_ref[...], preferred_element_type=F32))
        hb = _rms(y, g_ref[...]).astype(BF16)
        for c in range(n_ff):
            hid = jnp.dot(hb, wup_ref[:, c * fc:(c + 1) * fc], preferred_element_type=F32)
            act = jnp.square(jnp.maximum(hid, 0.0)).astype(BF16)
            y = y + jnp.dot(act, wdn_ref[c * fc:(c + 1) * fc, :], preferred_element_type=F32)
            after_chunk(c)
        y_ref[...] = y

    if side is None:
        (y_ref,) = rest
        ffn(lambda c: None)
        return

    (lq1_ref, lk1_ref, lq2_ref, lk2_ref, subg_ref, q8_ref, knew_ref, vnew_ref, ck_ref, cv_ref,
     y_ref, do_ref, ring, sems, probs, acc) = rest
    step = pl.program_id(0)
    n_steps = pl.num_programs(0)
    slots, cp, page_rows, _ = ring.shape
    n_pages = pt_ref.shape[1]
    n_chunks = n_pages // cp
    per_ff = n_chunks // n_ff
    look = slots
    b = step // 2

    def page_copy(phase, at_step, chunk, i):
        sample = side.first_sample + at_step // 2
        src = ck_ref if phase == 0 else cv_ref
        slot = chunk % slots
        return pltpu.make_async_copy(src.at[pt_ref[sample, chunk * cp + i]], ring.at[slot, i], sems.at[slot])

    def start_chunk(phase, at_step, chunk):
        for i in range(cp):
            page_copy(phase, at_step, chunk, i).start()

    def use_chunk(phase, chunk):
        cols = slice(chunk * cp * page_rows, (chunk + 1) * cp * page_rows)
        pages = ring[chunk % slots].reshape(cp * page_rows, ring.shape[-1]).astype(BF16)
        if phase == 0:
            probs[:, cols] = lax.dot_general(q8_ref[b], pages, (((1,), (1,)), ((), ())), preferred_element_type=F32)
        else:
            acc[...] += jnp.dot(probs[:, cols].astype(BF16), pages, preferred_element_type=F32)

    def after_chunk(phase, c):
        chunks = range(c * per_ff, (c + 1) * per_ff)
        for chunk in chunks:
            for i in range(cp):
                page_copy(phase, step, chunk, i).wait()
        for chunk in chunks:
            use_chunk(phase, chunk)
        for chunk in chunks:
            ahead = chunk + look
            if ahead < n_chunks:
                start_chunk(phase, step, ahead)
            else:
                @pl.when(step + 1 < n_steps)
                def _():
                    start_chunk(1 - phase, step + 1, ahead - n_chunks)

    @pl.when(step == 0)
    def _():
        for chunk in range(look):
            start_chunk(0, step, chunk)

    @pl.when(step % 2 == 0)
    def _():
        ffn(functools.partial(after_chunk, 0))

    @pl.when(step % 2 == 1)
    def _():
        e_new, inv = _decode_softmax(probs, q8_ref[b], knew_ref[b], n_pages * page_rows)
        acc[...] = jnp.zeros_like(acc)
        ffn(functools.partial(after_chunk, 1))
        vnew = vnew_ref[b]
        a = (acc[...] + e_new * jnp.concatenate([vnew, vnew], axis=0)) * inv
        lam = _lambda(lq1_ref, lk1_ref, lq2_ref, lk2_ref, side.lam_init)
        o = a[:H_DIFF] - lam * a[H_DIFF:]
        do_ref[b] = _sub_norm(o, subg_ref[...], side.lam_init)


def _mix_ffn(x, parts, w_o, g, w_up, w_dn, tm, side=None, decode_ops=None):
    rows, d = x.shape
    widths = tuple(p.shape[1] for p in parts)
    n_steps = rows // tm
    row = lambda width: pl.BlockSpec((tm, width), lambda i, *_: (i, 0))
    weights = ([] if w_o is None else [w_o]) + [g, w_up, w_dn]
    in_specs = [row(d)] + [row(w) for w in widths] + [_resident(w.shape) for w in weights]
    body = functools.partial(_mix_ffn_body, widths=widths, projected=w_o is None, side=side)
    if side is None:
        return pl.pallas_call(
            body, grid=(n_steps,), in_specs=in_specs, out_specs=row(d),
            out_shape=jax.ShapeDtypeStruct((rows, d), F32),
            compiler_params=_params(("parallel",)), name="mix_ffn",
        )(x, *parts, *weights)

    page_table, cache_k, cache_v, q8, knew8, vnew, lam_vecs, subg = decode_ops
    n_pages = page_table.shape[1]
    _, page_rows, width = cache_k.shape
    assert n_steps == 2 * side.n_samples and n_pages % DEC_CHUNK_PAGES == 0
    assert (n_pages // DEC_CHUNK_PAGES) % (w_up.shape[1] // FF_CHUNK) == 0
    assert (n_pages // DEC_CHUNK_PAGES) % DEC_SLOTS == 0
    lo, hi = side.first_sample, side.first_sample + side.n_samples
    local = [q8[lo:hi], knew8[lo:hi], vnew[lo:hi]]
    any_spec = pl.BlockSpec(memory_space=pl.ANY)
    grid_spec = pltpu.PrefetchScalarGridSpec(
        num_scalar_prefetch=1,
        grid=(n_steps,),
        in_specs=in_specs + [_resident(v.shape) for v in lam_vecs] + [_resident(subg.shape)]
        + [_resident(a.shape) for a in local] + [any_spec, any_spec],
        out_specs=[row(d), pl.BlockSpec((side.n_samples, H_DIFF, width), lambda *_: (0, 0, 0))],
        scratch_shapes=[pltpu.VMEM((DEC_SLOTS, DEC_CHUNK_PAGES, page_rows, width), F32),
                        pltpu.SemaphoreType.DMA((DEC_SLOTS,)),
                        pltpu.VMEM((2 * H_DIFF, n_pages * page_rows), F32),
                        pltpu.VMEM((2 * H_DIFF, width), F32)],
    )
    return pl.pallas_call(
        body, grid_spec=grid_spec,
        out_shape=[jax.ShapeDtypeStruct((rows, d), F32), jax.ShapeDtypeStruct((side.n_samples, H_DIFF, width), F32)],
        compiler_params=_params(("arbitrary",)), name="mix_ffn_decode",
    )(page_table, x, *parts, *weights, *lam_vecs, subg, *local, cache_k, cache_v)


def _gelu(x):
    return 0.5 * x * (1.0 + lax.erf(x * (2.0 ** -0.5)))


def _gmlp_body(x_ref, g_ref, w_ref, lng_ref, lnb_ref, ws_ref, bs_ref, wout_ref, mix_ref, *rest, e, single_token):
    if single_token:
        vn_ref, vbuf, a_ref = rest
    else:
        vbuf, a_ref = rest
    hb = _rms(x_ref[...], g_ref[...]).astype(BF16)
    rows = hb.shape[0]
    gc = e // GMLP_GROUPS
    pc = 2 * gc
    for pair in range(GMLP_GROUPS // 2):
        vbuf[:, pair * pc:(pair + 1) * pc] = _gelu(
            jnp.dot(hb, w_ref[:, e + pair * pc:e + (pair + 1) * pc], preferred_element_type=F32))
    rb = min(rows, GMLP_CHUNK)
    for r in range(rows // rb):
        rsl = slice(r * rb, (r + 1) * rb)
        v = vbuf[rsl, :]
        vc = v - jnp.mean(v, axis=-1, keepdims=True)
        vn = vc * lax.rsqrt(jnp.mean(vc * vc, axis=-1, keepdims=True) + EPS) * lng_ref[...] + lnb_ref[...]
        vbuf[rsl, :] = vn
        if single_token:
            vn_ref[rsl, :] = vn
    if not single_token:
        t = lax.broadcasted_iota(jnp.int32, (GMLP_CHUNK, GMLP_CHUNK), 0)
        s_ = lax.broadcasted_iota(jnp.int32, (GMLP_CHUNK, GMLP_CHUNK), 1)
        tril = t >= s_
    for grp in range(GMLP_GROUPS):
        cols = slice(grp * gc, (grp + 1) * gc)
        if grp % 2 == 0:
            u2 = _gelu(jnp.dot(hb, w_ref[:, grp * gc:grp * gc + pc], preferred_element_type=F32))
        u = u2[:, (grp % 2) * gc:(grp % 2 + 1) * gc]
        if single_token:
            a_ref[:, cols] = (u * (vbuf[:, cols] * ws_ref[grp][0:1, 0:1] + bs_ref[0:1, grp:grp + 1])).astype(BF16)
        else:
            wsg = jnp.where(tril, ws_ref[grp], 0.0).astype(BF16)
            bias = bs_ref[:, grp:grp + 1]
            for r in range(rows // GMLP_CHUNK):
                rsl = slice(r * GMLP_CHUNK, (r + 1) * GMLP_CHUNK)
                s = jnp.dot(wsg, vbuf[rsl, cols].astype(BF16), preferred_element_type=F32) + bias
                a_ref[rsl, cols] = (u[rsl] * s).astype(BF16)
    mix_ref[...] = jnp.dot(a_ref[...], wout_ref[...], preferred_element_type=F32)


def _gmlp(x, g, w, lng, lnb, ws, bs_t, w_out, tm, single_token):
    rows, d = x.shape
    e = w.shape[1] // 2
    row = lambda width: pl.BlockSpec((tm, width), lambda i: (i, 0))
    out_specs = [row(d)]
    out_shape = [jax.ShapeDtypeStruct((rows, d), F32)]
    if single_token:
        out_specs.append(row(e))
        out_shape.append(jax.ShapeDtypeStruct((rows, e), F32))
    return pl.pallas_call(
        functools.partial(_gmlp_body, e=e, single_token=single_token),
        grid=(rows // tm,),
        in_specs=[row(d), _resident(g.shape), _resident(w.shape), _resident(lng.shape), _resident(lnb.shape),
                  _resident(ws.shape), _resident(bs_t.shape), _resident(w_out.shape)],
        out_specs=out_specs,
        out_shape=out_shape,
        scratch_shapes=[pltpu.VMEM((tm, e), F32), pltpu.VMEM((tm, e), BF16)],
        compiler_params=_params(("parallel",)),
        name="gmlp_sample" if single_token else "gmlp_prompt",
    )(x, g, w, lng, lnb, ws, bs_t, w_out)


def _rope_tables(pos, d):
    inv = 1.0 / (ROPE_THETA ** (jnp.arange(0, d, 2, dtype=F32) / d))
    ang = pos.astype(F32)[:, None] * inv[None, :]
    c, s = jnp.cos(ang), jnp.sin(ang)
    reps = LANES // d
    return jnp.tile(jnp.concatenate([c, c], -1), (1, reps)), jnp.tile(jnp.concatenate([-s, s], -1), (1, reps))


def kernel(x_prompt, x_sample, state_ret, cache_k, cache_v, page_table, rms_mix_g, rms_ffn_g, w_ffn_up, w_ffn_down,
           w_in_even, w_out_even, q_norm_g, k_norm_g, lam_q1, lam_k1, lam_q2, lam_k2, subln_g,
           w_in_odd, ln_v_g, ln_v_b, w_s, b_s, w_out_odd):
    bp, sp, d = x_prompt.shape
    bs, ls, _ = x_sample.shape
    depth = rms_mix_g.shape[0]
    dk_ret, dv_ret = state_ret.shape[-2:]
    n_pool, page_size = cache_k.shape[1:3]
    dh = cache_k.shape[-1] // 2
    dv_diff = cache_v.shape[-1]
    assert dk_ret == LANES and 2 * dh == LANES and dv_diff == LANES and ls == 1
    ret_qk, ret_v = H_RET * dk_ret, H_RET * dv_ret
    diff_qk, diff_v = H_DIFF * 2 * dh, H_DIFF * dv_diff
    dims = (ret_qk, ret_v, diff_qk, diff_v, dh)
    past = page_table.shape[1] * page_size
    tm = ROW_TILE if (bp * sp) % ROW_TILE == 0 and sp % ROW_TILE == 0 else sp
    rows_s = bs * ls

    pos_p = jnp.arange(sp, dtype=jnp.int32)
    pos_s = jnp.tile(past + jnp.arange(ls, dtype=jnp.int32), bs)
    tabs_p = _rope_tables(pos_p, dk_ret) + _rope_tables(pos_p, dh)
    tabs_s = _rope_tables(pos_s, dk_ret) + _rope_tables(pos_s, dh)
    gidx = jnp.arange(2 * LANES) // dh
    grp = (gidx[:, None] == gidx[None, :]).astype(BF16)

    assert depth == 2, "the decode attention of layer 0 is spread over the prompt MLP calls of layers 0 and 1"
    yp = x_prompt.reshape(bp * sp, d)
    ys = x_sample.reshape(rows_s, d)
    ffn_w = [(rms_ffn_g[l][None], w_ffn_up[l].astype(BF16), w_ffn_down[l].astype(BF16)) for l in range(depth)]

    lam_init = 0.8 - 0.6 * math.exp(-0.3 * 0)
    g_mix = rms_mix_g[0][None]
    w_in = w_in_even[0]
    w_out = w_out_even[0].astype(BF16)
    qg = jnp.tile(q_norm_g[0], LANES // dh)[None]
    kg = jnp.tile(k_norm_g[0], LANES // dh)[None]
    lam_vecs = [lam_q1[0][None], lam_k1[0][None], lam_q2[0][None], lam_k2[0][None]]
    subg = subln_g[0][None]

    rq, rk, rv, gate, dq, dk_p, dv_p, dkb, dvb = _in_proj(yp, g_mix, w_in, tabs_p, sp // tm, qg, kg, grp, tm, dims)
    ro_p, s_fin = _retention_prompt(rq, rk, rv, gate, bp, sp, dk_ret, dv_ret)
    do_p = _attn_prompt(dq, dkb, dvb, lam_vecs, subg, bp, sp, dh, lam_init)

    rq, rk, rv, gate, dq, dk_s, dv_s, _, _ = _in_proj(ys, g_mix, w_in, tabs_s, 1, qg, kg, grp, rows_s, dims)
    s_new, ro_s = _retention_step(
        state_ret[0],
        rq.astype(F32).reshape(bs, H_RET, dk_ret, 1), rk.astype(F32).reshape(bs, H_RET, dk_ret, 1),
        rv.reshape(bs, H_RET, 1, dv_ret), gate.reshape(bs, H_RET, 1, dv_ret), 4 if bs % 4 == 0 else 1)
    q4 = dq.reshape(bs, H_DIFF, 2 * dh)
    half = (jnp.arange(2 * dh) < dh)[None, None, :]
    q8 = jnp.concatenate([jnp.where(half, q4, 0), jnp.where(half, 0, q4)], axis=1)
    k4 = dk_s
    decode_ops = (page_table,
                  cache_k[0].reshape(n_pool, page_size * H_DIFF, 2 * dh),
                  cache_v[0].reshape(n_pool, page_size * H_DIFF, dv_diff),
                  q8, jnp.concatenate([k4, k4], axis=1), dv_s.reshape(bs, H_DIFF, dv_diff), lam_vecs, subg)
    n_first = bs // 2

    yp, do_a = _mix_ffn(yp, [ro_p, do_p], w_out, *ffn_w[0], tm, _DecodeSide(0, n_first, lam_init), decode_ops)

    g_mix1 = rms_mix_g[1][None]
    w_in1 = w_in_odd[0].astype(BF16)
    w_out1 = w_out_odd[0].astype(BF16)
    lng, lnb = ln_v_g[0][None], ln_v_b[0][None]
    bs_t = b_s[0].T
    (mix,) = _gmlp(yp, g_mix1, w_in1, lng, lnb, w_s[0], bs_t, w_out1, min(tm, GMLP_TILE), False)
    yp, do_b = _mix_ffn(yp, [mix], None, *ffn_w[1], tm, _DecodeSide(n_first, bs - n_first, lam_init), decode_ops)

    do_s = jnp.concatenate([do_a, do_b], axis=0).astype(BF16)
    ys = _mix_ffn(ys, [ro_s.reshape(rows_s, ret_v), do_s.reshape(rows_s, diff_v)], w_out, *ffn_w[0], rows_s)
    mix, gv = _gmlp(ys, g_mix1, w_in1, lng, lnb, w_s[0], bs_t, w_out1, rows_s, True)
    ys = _mix_ffn(ys, [mix], None, *ffn_w[1], rows_s)

    return (yp.reshape(bp, sp, d), ys.reshape(bs, ls, d), s_fin[None], s_new[None],
            dk_p.reshape(1, bp, sp, H_DIFF, 2 * dh), dv_p.reshape(1, bp, sp, H_DIFF, dv_diff),
            dk_s.reshape(1, bs, ls, H_DIFF, 2 * dh), dv_s.reshape(1, bs, ls, H_DIFF, dv_diff),
            gv.reshape(1, bs, ls, -1))
```

```python
import functools
import math
from typing import NamedTuple

import jax
import jax.numpy as jnp
from jax import lax
from jax.experimental import pallas as pl
from jax.experimental.pallas import tpu as pltpu

F32 = jnp.float32
BF16 = jnp.bfloat16
EPS = 1e-6
ROPE_THETA = 10000.0
H_RET = 4
H_DIFF = 4
GMLP_GROUPS = 8
GMLP_CHUNK = 128
LANES = 128
VMEM_LIMIT_BYTES = 58 * 1024 * 1024
NEG_BIG = -1e30

ROW_TILE = 512
GMLP_TILE = 512
RET_CHUNK = 256
ATTN_TILE = 512
ATTN_HEADS = 2
FF_CHUNK = 512
DEC_CHUNK_PAGES = 8
DEC_SLOTS = 4
DEC_SOFTMAX_PIECE = 8192


def _params(semantics):
    return pltpu.CompilerParams(dimension_semantics=semantics, vmem_limit_bytes=VMEM_LIMIT_BYTES)


def _resident(shape):
    return pl.BlockSpec(shape, lambda *_: (0,) * len(shape), pipeline_mode=pl.Buffered(1))


def _rms(x, g):
    return x * lax.rsqrt(jnp.mean(x * x, axis=-1, keepdims=True) + EPS) * g


def _log_gamma(h):
    return math.log1p(-(2.0 ** (-5.0 - h)))


def _in_proj_body(x_ref, g_ref, w_ref, cr_ref, sr_ref, cd_ref, sd_ref, qg_ref, kg_ref, grp_ref,
                  rq_ref, rk_ref, rv_ref, gate_ref, dq_ref, dk_ref, dv_ref, dkb_ref, dvb_ref,
                  *, ret_qk, ret_v, diff_qk, diff_v, dh):
    hb = _rms(x_ref[...], g_ref[...]).astype(BF16)
    rows = hb.shape[0]

    def proj(lo, width):
        return jnp.dot(hb, w_ref[:, lo:lo + width].astype(BF16), preferred_element_type=F32)

    cr, sr = cr_ref[...], sr_ref[...]
    for col0, out_ref in ((0, rq_ref), (ret_qk, rk_ref)):
        p = proj(col0, ret_qk)
        for c in range(ret_qk // LANES):
            s = p[:, c * LANES:(c + 1) * LANES]
            out_ref[:, c * LANES:(c + 1) * LANES] = (s * cr + pltpu.roll(s, LANES // 2, 1) * sr).astype(BF16)
    rv_ref[...] = proj(2 * ret_qk, ret_v).astype(BF16)
    rg = proj(2 * ret_qk + ret_v, ret_v)
    gate_ref[...] = (rg * jax.nn.sigmoid(rg)).astype(BF16)

    cd, sd = cd_ref[...], sd_ref[...]
    lane = lax.broadcasted_iota(jnp.int32, (rows, LANES), 1)
    first_half = (lane & (dh // 2)) == 0
    grp = grp_ref[...]
    gw = grp.shape[0]
    base = 2 * ret_qk + 2 * ret_v

    def norm_rope(col0, gain):
        p = proj(col0, diff_qk)
        slabs = []
        for c in range(diff_qk // gw):
            s = p[:, c * gw:(c + 1) * gw]
            ssq = jnp.dot((s * s).astype(BF16), grp, preferred_element_type=F32)
            n = s * lax.rsqrt(ssq * (1.0 / dh) + EPS)
            for d in range(gw // LANES):
                t = n[:, d * LANES:(d + 1) * LANES] * gain
                partner = jnp.where(first_half, pltpu.roll(t, LANES - dh // 2, 1), pltpu.roll(t, dh // 2, 1))
                slabs.append(t * cd + partner * sd)
        return slabs

    q_scale = dh ** -0.5 * math.log2(math.e)
    for c, s in enumerate(norm_rope(base, qg_ref[...])):
        dq_ref[:, c * LANES:(c + 1) * LANES] = (s * q_scale).astype(BF16)
    for c, s in enumerate(norm_rope(base + diff_qk, kg_ref[...])):
        dk_ref[:, c, :] = s
        dkb_ref[:, c * LANES:(c + 1) * LANES] = s.astype(BF16)
    dv = proj(base + 2 * diff_qk, diff_v)
    for c in range(diff_v // LANES):
        dv_ref[:, c, :] = dv[:, c * LANES:(c + 1) * LANES]
    dvb_ref[...] = dv.astype(BF16)


def _in_proj(x, g, w, tabs, tab_blocks, qg, kg, grp, tm, dims):
    rows, d = x.shape
    ret_qk, ret_v, diff_qk, diff_v, dh = dims
    n = rows // tm
    row = lambda width: pl.BlockSpec((tm, width), lambda i: (i, 0))
    tab = pl.BlockSpec((tm, LANES), lambda i: (i % tab_blocks, 0))
    body = functools.partial(_in_proj_body, ret_qk=ret_qk, ret_v=ret_v, diff_qk=diff_qk, diff_v=diff_v, dh=dh)
    shapes = [((rows, ret_qk), BF16), ((rows, ret_qk), BF16), ((rows, ret_v), BF16), ((rows, ret_v), BF16),
              ((rows, diff_qk), BF16), ((rows, diff_qk // LANES, LANES), F32), ((rows, diff_v // LANES, LANES), F32),
              ((rows, diff_qk), BF16), ((rows, diff_v), BF16)]
    out_spec = lambda s: row(s[1]) if len(s) == 2 else pl.BlockSpec((tm,) + s[1:], lambda i: (i, 0, 0))
    return pl.pallas_call(
        body,
        grid=(n,),
        in_specs=[row(d), _resident(g.shape), _resident(w.shape), tab, tab, tab, tab,
                  _resident(qg.shape), _resident(kg.shape), _resident(grp.shape)],
        out_specs=[out_spec(s) for s, _ in shapes],
        out_shape=[jax.ShapeDtypeStruct(s, t) for s, t in shapes],
        compiler_params=_params(("parallel",)),
        name="in_proj",
    )(x, g, w, *tabs, qg, kg, grp)


def _retention_body(q_ref, k_ref, v_ref, gate_ref, ro_ref, sfin_ref, state, dmask, *, dk, dv):
    b, c = pl.program_id(0), pl.program_id(1)
    chunk = q_ref.shape[0]
    scale = dk ** -0.5

    @pl.when((b == 0) & (c == 0))
    def _():
        i = lax.broadcasted_iota(jnp.int32, (chunk, chunk), 0)
        j = lax.broadcasted_iota(jnp.int32, (chunk, chunk), 1)
        causal = i >= j
        diff = jnp.where(causal, i - j, 0).astype(F32)
        for h in range(H_RET):
            dmask[h] = jnp.where(causal, jnp.exp(diff * _log_gamma(h)), 0.0) * scale

    @pl.when(c == 0)
    def _():
        state[...] = jnp.zeros_like(state)

    idx = lax.broadcasted_iota(jnp.int32, (chunk, 1), 0).astype(F32)
    for h in range(H_RET):
        lg = _log_gamma(h)
        q = q_ref[:, h * dk:(h + 1) * dk]
        k = k_ref[:, h * dk:(h + 1) * dk]
        v = v_ref[:, h * dv:(h + 1) * dv]
        s_prev = state[h]
        scores = lax.dot_general(q, k, (((1,), (1,)), ((), ())), preferred_element_type=F32) * dmask[h]
        inner = jnp.dot(scores.astype(BF16), v, preferred_element_type=F32)
        cross = jnp.dot(q, s_prev.astype(BF16), preferred_element_type=F32) * jnp.exp((idx + 1.0) * lg)
        o = inner + cross
        zeta = jnp.exp((chunk - 1.0 - idx) * lg) * scale
        kz_t = (k.astype(F32) * zeta).T.astype(BF16)
        state[h] = math.exp(chunk * lg) * s_prev + jnp.dot(kz_t, v, preferred_element_type=F32)
        oc = o - jnp.mean(o, axis=-1, keepdims=True)
        on = oc * lax.rsqrt(jnp.mean(oc * oc, axis=-1, keepdims=True) + EPS)
        ro_ref[:, h * dv:(h + 1) * dv] = (on * gate_ref[:, h * dv:(h + 1) * dv].astype(F32)).astype(BF16)

    @pl.when(c == pl.num_programs(1) - 1)
    def _():
        sfin_ref[0] = state[...]


def _retention_prompt(rq, rk, rv, gate, batch, seq, dk, dv):
    chunk = RET_CHUNK if seq % RET_CHUNK == 0 else seq
    nc = seq // chunk
    blk = lambda width: pl.BlockSpec((chunk, width), lambda b, c: (b * nc + c, 0))
    return pl.pallas_call(
        functools.partial(_retention_body, dk=dk, dv=dv),
        grid=(batch, nc),
        in_specs=[blk(H_RET * dk), blk(H_RET * dk), blk(H_RET * dv), blk(H_RET * dv)],
        out_specs=[blk(H_RET * dv), pl.BlockSpec((1, H_RET, dk, dv), lambda b, c: (b, 0, 0, 0))],
        out_shape=[jax.ShapeDtypeStruct((batch * seq, H_RET * dv), BF16),
                   jax.ShapeDtypeStruct((batch, H_RET, dk, dv), F32)],
        scratch_shapes=[pltpu.VMEM((H_RET, dk, dv), F32), pltpu.VMEM((H_RET, chunk, chunk), F32)],
        compiler_params=_params(("arbitrary", "arbitrary")),
        name="retention_prompt",
    )(rq, rk, rv, gate)


def _retention_step_body(s_ref, q_ref, k_ref, v_ref, gate_ref, snew_ref, ro_ref, *, dk):
    scale = dk ** -0.5
    for h in range(H_RET):
        s_new = math.exp(_log_gamma(h)) * s_ref[:, h] + (k_ref[:, h] * scale) * v_ref[:, h]
        snew_ref[:, h] = s_new
        o = jnp.sum(q_ref[:, h] * s_new, axis=1, keepdims=True)
        oc = o - jnp.mean(o, axis=-1, keepdims=True)
        on = oc * lax.rsqrt(jnp.mean(oc * oc, axis=-1, keepdims=True) + EPS)
        ro_ref[:, h] = (on * gate_ref[:, h].astype(F32)).astype(BF16)


def _retention_step(state, q_col, k_col, v_row, gate_row, bt):
    nb, _, dk, dv = state.shape
    col = pl.BlockSpec((bt, H_RET, dk, 1), lambda i: (i, 0, 0, 0))
    rowb = pl.BlockSpec((bt, H_RET, 1, dv), lambda i: (i, 0, 0, 0))
    st = pl.BlockSpec((bt, H_RET, dk, dv), lambda i: (i, 0, 0, 0))
    return pl.pallas_call(
        functools.partial(_retention_step_body, dk=dk),
        grid=(nb // bt,),
        in_specs=[st, col, col, rowb, rowb],
        out_specs=[st, rowb],
        out_shape=[jax.ShapeDtypeStruct(state.shape, F32), jax.ShapeDtypeStruct((nb, H_RET, 1, dv), BF16)],
        compiler_params=_params(("parallel",)),
        name="retention_step",
    )(state, q_col, k_col, v_row, gate_row)


def _lambda(lq1_ref, lk1_ref, lq2_ref, lk2_ref, lam_init):
    a = jnp.sum(lq1_ref[...] * lk1_ref[...], axis=-1, keepdims=True)
    b = jnp.sum(lq2_ref[...] * lk2_ref[...], axis=-1, keepdims=True)
    return jnp.exp(a) - jnp.exp(b) + lam_init


def _sub_norm(o, subg, lam_init):
    return o * lax.rsqrt(jnp.mean(o * o, axis=-1, keepdims=True) + EPS) * subg * (1.0 - lam_init)


def _attn_prompt_body(lq1_ref, lk1_ref, lq2_ref, lk2_ref, subg_ref, q_ref, k_ref, v_ref, o_ref, acc_ref, s_ref,
                      *, dh, lam_init):
    i = pl.program_id(2)
    t = q_ref.shape[0]
    heads = acc_ref.shape[0]
    dv = width = 2 * dh
    lane = lax.broadcasted_iota(jnp.int32, (t, width), 1)
    zero = jnp.zeros((t, width), BF16)
    row = lax.broadcasted_iota(jnp.int32, (2 * t, t), 0)
    col = lax.broadcasted_iota(jnp.int32, (2 * t, t), 1)
    on_or_below = col <= jnp.where(row >= t, row - t, row)
    ones = jnp.ones((t, LANES), BF16)

    def head_cols(hh):
        return slice(hh * width, (hh + 1) * width)

    def stacked_q(hh):
        q = q_ref[:, head_cols(hh)]
        return jnp.concatenate([jnp.where(lane < dh, q, zero), jnp.where(lane >= dh, q, zero)], axis=0)

    q2 = [stacked_q(hh) for hh in range(heads)]

    def key_rows(j):
        return pl.ds(pl.multiple_of(j * t, t), t)

    def scores(hh, j):
        return lax.dot_general(q2[hh], k_ref[key_rows(j), head_cols(hh)], (((1,), (1,)), ((), ())),
                               preferred_element_type=F32)

    def lane_max(s, m):
        for c in range(t // LANES):
            m = jnp.maximum(m, s[:, c * LANES:(c + 1) * LANES])
        return m

    def accumulate(hh, j, s, mrow):
        p = jnp.exp2(s - mrow).astype(BF16)
        v1 = jnp.concatenate([v_ref[key_rows(j), head_cols(hh)], ones], axis=1)
        acc_ref[hh] += jnp.dot(p, v1, preferred_element_type=F32)

    def first_pass(j, ms):
        out = []
        for hh in range(heads):
            s = scores(hh, j)
            s_ref[hh, :, key_rows(j)] = s
            out.append(lane_max(s, ms[hh]))
        return tuple(out)

    s_diag = [jnp.where(on_or_below, scores(hh, i), NEG_BIG) for hh in range(heads)]
    ms = lax.fori_loop(0, i, first_pass,
                       tuple(lane_max(s, jnp.full((2 * t, LANES), NEG_BIG, F32)) for s in s_diag))
    mrow = [jnp.max(m, axis=-1, keepdims=True) for m in ms]
    for hh in range(heads):
        acc_ref[hh] = jnp.zeros(acc_ref.shape[1:], F32)
        accumulate(hh, i, s_diag[hh], mrow[hh])

    def second_pass(j, carry):
        for hh in range(heads):
            accumulate(hh, j, s_ref[hh, :, key_rows(j)], mrow[hh])
        return carry

    lax.fori_loop(0, i, second_pass, 0)
    lam = _lambda(lq1_ref, lk1_ref, lq2_ref, lk2_ref, lam_init)
    for hh in range(heads):
        acc = acc_ref[hh]
        o_all = acc[:, :dv] / acc[:, dv:]
        o = o_all[:t] - lam * o_all[t:]
        o_ref[:, head_cols(hh)] = _sub_norm(o, subg_ref[...], lam_init).astype(BF16)


def _attn_prompt(dq, dkb, dvb, lam_vecs, subg, batch, seq, dh, lam_init):
    t = ATTN_TILE if seq % ATTN_TILE == 0 else seq
    nq = seq // t
    width = 2 * dh
    hp = ATTN_HEADS if H_DIFF % ATTN_HEADS == 0 else 1
    vec = [_resident(v.shape) for v in lam_vecs]
    return pl.pallas_call(
        functools.partial(_attn_prompt_body, dh=dh, lam_init=lam_init),
        grid=(batch, H_DIFF // hp, nq),
        in_specs=vec + [_resident(subg.shape),
                        pl.BlockSpec((t, hp * width), lambda b, h, i: (b * nq + i, h)),
                        pl.BlockSpec((seq, hp * width), lambda b, h, i: (b, h)),
                        pl.BlockSpec((seq, hp * width), lambda b, h, i: (b, h))],
        out_specs=pl.BlockSpec((t, hp * width), lambda b, h, i: (b * nq + i, h)),
        out_shape=jax.ShapeDtypeStruct((batch * seq, H_DIFF * width), BF16),
        scratch_shapes=[pltpu.VMEM((hp, 2 * t, 2 * width), F32), pltpu.VMEM((hp, 2 * t, seq), F32)],
        compiler_params=_params(("parallel", "parallel", "arbitrary")),
        name="diff_attn_prompt",
    )(*lam_vecs, subg, dq, dkb, dvb)


class _DecodeSide(NamedTuple):
    first_sample: int
    n_samples: int
    lam_init: float


def _decode_softmax(probs, q8, knew, n_cols):
    piece = DEC_SOFTMAX_PIECE if n_cols % DEC_SOFTMAX_PIECE == 0 else n_cols
    lane = lax.broadcasted_iota(jnp.int32, (2 * H_DIFF, piece), 1)
    row = lax.broadcasted_iota(jnp.int32, (2 * H_DIFF, piece), 0)
    own_head = (lane & (H_DIFF - 1)) == (row & (H_DIFF - 1))
    s_new = jnp.sum(q8.astype(F32) * knew, axis=-1, keepdims=True)
    mx = s_new
    for c in range(n_cols // piece):
        s = jnp.where(own_head, probs[:, c * piece:(c + 1) * piece], NEG_BIG)
        mx = jnp.maximum(mx, jnp.max(s, axis=-1, keepdims=True))
    e_new = jnp.exp2(s_new - mx)
    total = e_new
    for c in range(n_cols // piece):
        e = jnp.exp2(jnp.where(own_head, probs[:, c * piece:(c + 1) * piece], NEG_BIG) - mx)
        probs[:, c * piece:(c + 1) * piece] = e
        total = total + jnp.sum(e, axis=-1, keepdims=True)
    return e_new, 1.0 / total


def _mix_ffn_body(*refs, widths, projected, side):
    if side is not None:
        pt_ref, refs = refs[0], refs[1:]
    n = len(widths)
    x_ref = refs[0]
    a_refs = refs[1:1 + n]
    if projected:
        wo_ref = None
        g_ref, wup_ref, wdn_ref = refs[1 + n:4 + n]
        rest = refs[4 + n:]
    else:
        wo_ref, g_ref, wup_ref, wdn_ref = refs[1 + n:5 + n]
        rest = refs[5 + n:]
    d_ff = wup_ref.shape[1]
    fc = FF_CHUNK if d_ff % FF_CHUNK == 0 else d_ff
    n_ff = d_ff // fc

    def ffn(after_chunk):
        a = a_refs[0][...] if n == 1 else jnp.concatenate([r[...] for r in a_refs], axis=-1)
        y = x_ref[...] + (a if projected else jnp.dot(a, wo_ref[...], preferred_element_type=F32))
        hb = _rms(y, g_ref[...]).astype(BF16)
        for c in range(n_ff):
            hid = jnp.dot(hb, wup_ref[:, c * fc:(c + 1) * fc], preferred_element_type=F32)
            act = jnp.square(jnp.maximum(hid, 0.0)).astype(BF16)
            y = y + jnp.dot(act, wdn_ref[c * fc:(c + 1) * fc, :], preferred_element_type=F32)
            after_chunk(c)
        y_ref[...] = y

    if side is None:
        (y_ref,) = rest
        ffn(lambda c: None)
        return

    (lq1_ref, lk1_ref, lq2_ref, lk2_ref, subg_ref, q8_ref, knew_ref, vnew_ref, ck_ref, cv_ref,
     y_ref, do_ref, ring, sems, probs, acc) = rest
    step = pl.program_id(0)
    n_steps = pl.num_programs(0)
    slots, cp, page_rows, _ = ring.shape
    n_pages = pt_ref.shape[1]
    n_chunks = n_pages // cp
    per_ff = n_chunks // n_ff
    look = slots
    b = step // 2

    def page_copy(phase, at_step, chunk, i):
        sample = side.first_sample + at_step // 2
        src = ck_ref if phase == 0 else cv_ref
        slot = chunk % slots
        return pltpu.make_async_copy(src.at[pt_ref[sample, chunk * cp + i]], ring.at[slot, i], sems.at[slot])

    def start_chunk(phase, at_step, chunk):
        for i in range(cp):
            page_copy(phase, at_step, chunk, i).start()

    def use_chunk(phase, chunk):
        cols = slice(chunk * cp * page_rows, (chunk + 1) * cp * page_rows)
        pages = ring[chunk % slots].reshape(cp * page_rows, ring.shape[-1]).astype(BF16)
        if phase == 0:
            probs[:, cols] = lax.dot_general(q8_ref[b], pages, (((1,), (1,)), ((), ())), preferred_element_type=F32)
        else:
            acc[...] += jnp.dot(probs[:, cols].astype(BF16), pages, preferred_element_type=F32)

    def after_chunk(phase, c):
        chunks = range(c * per_ff, (c + 1) * per_ff)
        for chunk in chunks:
            for i in range(cp):
                page_copy(phase, step, chunk, i).wait()
        for chunk in chunks:
            use_chunk(phase, chunk)
        for chunk in chunks:
            ahead = chunk + look
            if ahead < n_chunks:
                start_chunk(phase, step, ahead)
            else:
                @pl.when(step + 1 < n_steps)
                def _():
                    start_chunk(1 - phase, step + 1, ahead - n_chunks)

    @pl.when(step == 0)
    def _():
        for chunk in range(look):
            start_chunk(0, step, chunk)

    @pl.when(step % 2 == 0)
    def _():
        ffn(functools.partial(after_chunk, 0))

    @pl.when(step % 2 == 1)
    def _():
        e_new, inv = _decode_softmax(probs, q8_ref[b], knew_ref[b], n_pages * page_rows)
        acc[...] = jnp.zeros_like(acc)
        ffn(functools.partial(after_chunk, 1))
        vnew = vnew_ref[b]
        a = (acc[...] + e_new * jnp.concatenate([vnew, vnew], axis=0)) * inv
        lam = _lambda(lq1_ref, lk1_ref, lq2_ref, lk2_ref, side.lam_init)
        o = a[:H_DIFF] - lam * a[H_DIFF:]
        do_ref[b] = _sub_norm(o, subg_ref[...], side.lam_init)


def _mix_ffn(x, parts, w_o, g, w_up, w_dn, tm, side=None, decode_ops=None):
    rows, d = x.shape
    widths = tuple(p.shape[1] for p in parts)
    n_steps = rows // tm
    row = lambda width: pl.BlockSpec((tm, width), lambda i, *_: (i, 0))
    weights = ([] if w_o is None else [w_o]) + [g, w_up, w_dn]
    in_specs = [row(d)] + [row(w) for w in widths] + [_resident(w.shape) for w in weights]
    body = functools.partial(_mix_ffn_body, widths=widths, projected=w_o is None, side=side)
    if side is None:
        return pl.pallas_call(
            body, grid=(n_steps,), in_specs=in_specs, out_specs=row(d),
            out_shape=jax.ShapeDtypeStruct((rows, d), F32),
            compiler_params=_params(("parallel",)), name="mix_ffn",
        )(x, *parts, *weights)

    page_table, cache_k, cache_v, q8, knew8, vnew, lam_vecs, subg = decode_ops
    n_pages = page_table.shape[1]
    _, page_rows, width = cache_k.shape
    assert n_steps == 2 * side.n_samples and n_pages % DEC_CHUNK_PAGES == 0
    assert (n_pages // DEC_CHUNK_PAGES) % (w_up.shape[1] // FF_CHUNK) == 0
    assert (n_pages // DEC_CHUNK_PAGES) % DEC_SLOTS == 0
    lo, hi = side.first_sample, side.first_sample + side.n_samples
    local = [q8[lo:hi], knew8[lo:hi], vnew[lo:hi]]
    any_spec = pl.BlockSpec(memory_space=pl.ANY)
    grid_spec = pltpu.PrefetchScalarGridSpec(
        num_scalar_prefetch=1,
        grid=(n_steps,),
        in_specs=in_specs + [_resident(v.shape) for v in lam_vecs] + [_resident(subg.shape)]
        + [_resident(a.shape) for a in local] + [any_spec, any_spec],
        out_specs=[row(d), pl.BlockSpec((side.n_samples, H_DIFF, width), lambda *_: (0, 0, 0))],
        scratch_shapes=[pltpu.VMEM((DEC_SLOTS, DEC_CHUNK_PAGES, page_rows, width), F32),
                        pltpu.SemaphoreType.DMA((DEC_SLOTS,)),
                        pltpu.VMEM((2 * H_DIFF, n_pages * page_rows), F32),
                        pltpu.VMEM((2 * H_DIFF, width), F32)],
    )
    return pl.pallas_call(
        body, grid_spec=grid_spec,
        out_shape=[jax.ShapeDtypeStruct((rows, d), F32), jax.ShapeDtypeStruct((side.n_samples, H_DIFF, width), F32)],
        compiler_params=_params(("arbitrary",)), name="mix_ffn_decode",
    )(page_table, x, *parts, *weights, *lam_vecs, subg, *local, cache_k, cache_v)


def _gelu(x):
    return 0.5 * x * (1.0 + lax.erf(x * (2.0 ** -0.5)))


def _gmlp_body(x_ref, g_ref, w_ref, lng_ref, lnb_ref, ws_ref, bs_ref, wout_ref, mix_ref, *rest, e, single_token):
    if single_token:
        vn_ref, vbuf, a_ref = rest
    else:
        vbuf, a_ref = rest
    hb = _rms(x_ref[...], g_ref[...]).astype(BF16)
    rows = hb.shape[0]
    gc = e // GMLP_GROUPS
    pc = 2 * gc
    for pair in range(GMLP_GROUPS // 2):
        vbuf[:, pair * pc:(pair + 1) * pc] = _gelu(
            jnp.dot(hb, w_ref[:, e + pair * pc:e + (pair + 1) * pc], preferred_element_type=F32))
    rb = min(rows, GMLP_CHUNK)
    for r in range(rows // rb):
        rsl = slice(r * rb, (r + 1) * rb)
        v = vbuf[rsl, :]
        vc = v - jnp.mean(v, axis=-1, keepdims=True)
        vn = vc * lax.rsqrt(jnp.mean(vc * vc, axis=-1, keepdims=True) + EPS) * lng_ref[...] + lnb_ref[...]
        vbuf[rsl, :] = vn
        if single_token:
            vn_ref[rsl, :] = vn
    if not single_token:
        t = lax.broadcasted_iota(jnp.int32, (GMLP_CHUNK, GMLP_CHUNK), 0)
        s_ = lax.broadcasted_iota(jnp.int32, (GMLP_CHUNK, GMLP_CHUNK), 1)
        tril = t >= s_
    for grp in range(GMLP_GROUPS):
        cols = slice(grp * gc, (grp + 1) * gc)
        if grp % 2 == 0:
            u2 = _gelu(jnp.dot(hb, w_ref[:, grp * gc:grp * gc + pc], preferred_element_type=F32))
        u = u2[:, (grp % 2) * gc:(grp % 2 + 1) * gc]
        if single_token:
            a_ref[:, cols] = (u * (vbuf[:, cols] * ws_ref[grp][0:1, 0:1] + bs_ref[0:1, grp:grp + 1])).astype(BF16)
        else:
            wsg = jnp.where(tril, ws_ref[grp], 0.0).astype(BF16)
            bias = bs_ref[:, grp:grp + 1]
            for r in range(rows // GMLP_CHUNK):
                rsl = slice(r * GMLP_CHUNK, (r + 1) * GMLP_CHUNK)
                s = jnp.dot(wsg, vbuf[rsl, cols].astype(BF16), preferred_element_type=F32) + bias
                a_ref[rsl, cols] = (u[rsl] * s).astype(BF16)
    mix_ref[...] = jnp.dot(a_ref[...], wout_ref[...], preferred_element_type=F32)


def _gmlp(x, g, w, lng, lnb, ws, bs_t, w_out, tm, single_token):
    rows, d = x.shape
    e = w.shape[1] // 2
    row = lambda width: pl.BlockSpec((tm, width), lambda i: (i, 0))
    out_specs = [row(d)]
    out_shape = [jax.ShapeDtypeStruct((rows, d), F32)]
    if single_token:
        out_specs.append(row(e))
        out_shape.append(jax.ShapeDtypeStruct((rows, e), F32))
    return pl.pallas_call(
        functools.partial(_gmlp_body, e=e, single_token=single_token),
        grid=(rows // tm,),
        in_specs=[row(d), _resident(g.shape), _resident(w.shape), _resident(lng.shape), _resident(lnb.shape),
                  _resident(ws.shape), _resident(bs_t.shape), _resident(w_out.shape)],
        out_specs=out_specs,
        out_shape=out_shape,
        scratch_shapes=[pltpu.VMEM((tm, e), F32), pltpu.VMEM((tm, e), BF16)],
        compiler_params=_params(("parallel",)),
        name="gmlp_sample" if single_token else "gmlp_prompt",
    )(x, g, w, lng, lnb, ws, bs_t, w_out)


def _rope_tables(pos, d):
    inv = 1.0 / (ROPE_THETA ** (jnp.arange(0, d, 2, dtype=F32) / d))
    ang = pos.astype(F32)[:, None] * inv[None, :]
    c, s = jnp.cos(ang), jnp.sin(ang)
    reps = LANES // d
    return jnp.tile(jnp.concatenate([c, c], -1), (1, reps)), jnp.tile(jnp.concatenate([-s, s], -1), (1, reps))


def kernel(x_prompt, x_sample, state_ret, cache_k, cache_v, page_table, rms_mix_g, rms_ffn_g, w_ffn_up, w_ffn_down,
           w_in_even, w_out_even, q_norm_g, k_norm_g, lam_q1, lam_k1, lam_q2, lam_k2, subln_g,
           w_in_odd, ln_v_g, ln_v_b, w_s, b_s, w_out_odd):
    bp, sp, d = x_prompt.shape
    bs, ls, _ = x_sample.shape
    depth = rms_mix_g.shape[0]
    dk_ret, dv_ret = state_ret.shape[-2:]
    n_pool, page_size = cache_k.shape[1:3]
    dh = cache_k.shape[-1] // 2
    dv_diff = cache_v.shape[-1]
    assert dk_ret == LANES and 2 * dh == LANES and dv_diff == LANES and ls == 1
    ret_qk, ret_v = H_RET * dk_ret, H_RET * dv_ret
    diff_qk, diff_v = H_DIFF * 2 * dh, H_DIFF * dv_diff
    dims = (ret_qk, ret_v, diff_qk, diff_v, dh)
    past = page_table.shape[1] * page_size
    tm = ROW_TILE if (bp * sp) % ROW_TILE == 0 and sp % ROW_TILE == 0 else sp
    rows_s = bs * ls

    pos_p = jnp.arange(sp, dtype=jnp.int32)
    pos_s = jnp.tile(past + jnp.arange(ls, dtype=jnp.int32), bs)
    tabs_p = _rope_tables(pos_p, dk_ret) + _rope_tables(pos_p, dh)
    tabs_s = _rope_tables(pos_s, dk_ret) + _rope_tables(pos_s, dh)
    gidx = jnp.arange(2 * LANES) // dh
    grp = (gidx[:, None] == gidx[None, :]).astype(BF16)

    assert depth == 2, "the decode attention of layer 0 is spread over the prompt MLP calls of layers 0 and 1"
    yp = x_prompt.reshape(bp * sp, d)
    ys = x_sample.reshape(rows_s, d)
    ffn_w = [(rms_ffn_g[l][None], w_ffn_up[l].astype(BF16), w_ffn_down[l].astype(BF16)) for l in range(depth)]

    lam_init = 0.8 - 0.6 * math.exp(-0.3 * 0)
    g_mix = rms_mix_g[0][None]
    w_in = w_in_even[0]
    w_out = w_out_even[0].astype(BF16)
    qg = jnp.tile(q_norm_g[0], LANES // dh)[None]
    kg = jnp.tile(k_norm_g[0], LANES // dh)[None]
    lam_vecs = [lam_q1[0][None], lam_k1[0][None], lam_q2[0][None], lam_k2[0][None]]
    subg = subln_g[0][None]

    rq, rk, rv, gate, dq, dk_p, dv_p, dkb, dvb = _in_proj(yp, g_mix, w_in, tabs_p, sp // tm, qg, kg, grp, tm, dims)
    ro_p, s_fin = _retention_prompt(rq, rk, rv, gate, bp, sp, dk_ret, dv_ret)
    do_p = _attn_prompt(dq, dkb, dvb, lam_vecs, subg, bp, sp, dh, lam_init)

    rq, rk, rv, gate, dq, dk_s, dv_s, _, _ = _in_proj(ys, g_mix, w_in, tabs_s, 1, qg, kg, grp, rows_s, dims)
    s_new, ro_s = _retention_step(
        state_ret[0],
        rq.astype(F32).reshape(bs, H_RET, dk_ret, 1), rk.astype(F32).reshape(bs, H_RET, dk_ret, 1),
        rv.reshape(bs, H_RET, 1, dv_ret), gate.reshape(bs, H_RET, 1, dv_ret), 4 if bs % 4 == 0 else 1)
    q4 = dq.reshape(bs, H_DIFF, 2 * dh)
    half = (jnp.arange(2 * dh) < dh)[None, None, :]
    q8 = jnp.concatenate([jnp.where(half, q4, 0), jnp.where(half, 0, q4)], axis=1)
    k4 = dk_s
    decode_ops = (page_table,
                  cache_k[0].reshape(n_pool, page_size * H_DIFF, 2 * dh),
                  cache_v[0].reshape(n_pool, page_size * H_DIFF, dv_diff),
                  q8, jnp.concatenate([k4, k4], axis=1), dv_s.reshape(bs, H_DIFF, dv_diff), lam_vecs, subg)
    n_first = bs // 2

    yp, do_a = _mix_ffn(yp, [ro_p, do_p], w_out, *ffn_w[0], tm, _DecodeSide(0, n_first, lam_init), decode_ops)

    g_mix1 = rms_mix_g[1][None]
    w_in1 = w_in_odd[0].astype(BF16)
    w_out1 = w_out_odd[0].astype(BF16)
    lng, lnb = ln_v_g[0][None], ln_v_b[0][None]
    bs_t = b_s[0].T
    (mix,) = _gmlp(yp, g_mix1, w_in1, lng, lnb, w_s[0], bs_t, w_out1, min(tm, GMLP_TILE), False)
    yp, do_b = _mix_ffn(yp, [mix], None, *ffn_w[1], tm, _DecodeSide(n_first, bs - n_first, lam_init), decode_ops)

    do_s = jnp.concatenate([do_a, do_b], axis=0).astype(BF16)
    ys = _mix_ffn(ys, [ro_s.reshape(rows_s, ret_v), do_s.reshape(rows_s, diff_v)], w_out, *ffn_w[0], rows_s)
    mix, gv = _gmlp(ys, g_mix1, w_in1, lng, lnb, w_s[0], bs_t, w_out1, rows_s, True)
    ys = _mix_ffn(ys, [mix], None, *ffn_w[1], rows_s)

    return (yp.reshape(bp, sp, d), ys.reshape(bs, ls, d), s_fin[None], s_new[None],
            dk_p.reshape(1, bp, sp, H_DIFF, 2 * dh), dv_p.reshape(1, bp, sp, H_DIFF, dv_diff),
            dk_s.reshape(1, bs, ls, H_DIFF, 2 * dh), dv_s.reshape(1, bs, ls, H_DIFF, dv_diff),
            gv.reshape(1, bs, ls, -1))
```
